```python
import math
import jax, jax.numpy as jnp
from jax import lax
import numpy as np

D_MODEL = 2048
BATCH = 4
SEQ = 2048
DEPTH = 2

N_BRANCH = 4
BRANCH_WIDTH = D_MODEL // 4
GM_GROUPS = 4
GM_CHUNK = 128
GM_GROUP_DIM = BRANCH_WIDTH // GM_GROUPS
DA_HEADS = 4
DA_QK_DIM = BRANCH_WIDTH // (2 * DA_HEADS)
DA_V_DIM = 2 * DA_QK_DIM
DA_ROT_DIM = DA_QK_DIM // 4
FA_HEADS = 4
FA_HEAD_DIM = BRANCH_WIDTH // FA_HEADS
POOL_WINDOWS = (2, 4, 8, 16)
POOL_GROUPS = 4
POOL_GROUP_DIM = BRANCH_WIDTH // POOL_GROUPS
FFN_DIM = 4 * D_MODEL
ROPE_THETA = 500000.0
Q_BLOCK = 128
NORM_EPS = 1e-6

A_U = 0
A_V = A_U + BRANCH_WIDTH
B_Q = A_V + BRANCH_WIDTH
B_K = B_Q + 2 * DA_HEADS * DA_QK_DIM
B_V = B_K + 2 * DA_HEADS * DA_QK_DIM
C_Q = B_V + DA_HEADS * DA_V_DIM
C_K = C_Q + FA_HEADS * FA_HEAD_DIM
C_V = C_K + FA_HEADS * FA_HEAD_DIM
C_F = C_V + FA_HEADS * FA_HEAD_DIM
D_H = C_F + FA_HEADS
GATE = D_H + BRANCH_WIDTH
IN_COLS = GATE + N_BRANCH * D_MODEL

kernel_name = "hybrid_gated_parallel_mixers"


def rms_norm(x, g):
    xf = x.astype(jnp.float32)
    y = xf * lax.rsqrt(jnp.mean(xf * xf, axis=-1, keepdims=True) + NORM_EPS)
    return (y * g.astype(jnp.float32)).astype(x.dtype)


def rotary_tables(positions, rot_dim):
    inv = 1.0 / (ROPE_THETA ** (jnp.arange(0, rot_dim, 2, dtype=jnp.float32) / rot_dim))
    ang = positions.astype(jnp.float32)[..., None] * inv
    return jnp.cos(ang), jnp.sin(ang)


def apply_partial_rotary(x, cos, sin):
    half = cos.shape[-1]
    xf = x.astype(jnp.float32)
    x1, x2, xp = xf[..., :half], xf[..., half:2 * half], xf[..., 2 * half:]
    c, s = cos[:, :, None, :], sin[:, :, None, :]
    out = jnp.concatenate([x1 * c - x2 * s, x1 * s + x2 * c, xp], axis=-1)
    return out.astype(x.dtype)


def chunked_spatial_gating(u, v, ln_g, ln_b, w_s, b_s):
    B, S, _ = v.shape
    vf = v.astype(jnp.float32)
    mu = jnp.mean(vf, axis=-1, keepdims=True)
    var = jnp.mean(jnp.square(vf - mu), axis=-1, keepdims=True)
    vn = ((vf - mu) * lax.rsqrt(var + NORM_EPS) * ln_g.astype(jnp.float32)
          + ln_b.astype(jnp.float32)).astype(v.dtype)
    vn = vn.reshape(B, S // GM_CHUNK, GM_CHUNK, GM_GROUPS, GM_GROUP_DIM)
    causal = jnp.tril(jnp.ones((GM_CHUNK, GM_CHUNK), dtype=w_s.dtype))
    mixed = jnp.einsum('gts,bnsgc->bntgc', w_s * causal, vn) + b_s.T[:, :, None]
    return u * mixed.reshape(B, S, GM_GROUPS * GM_GROUP_DIM)


def differential_attention(q, k, v, lam, subln_g, lambda_init):
    B, S, H, _, dk = q.shape
    dv = v.shape[-1]
    nb = S // Q_BLOCK
    kh = k.transpose(0, 2, 3, 1, 4)
    vh = v.transpose(0, 2, 1, 3)
    q_blocks = q.transpose(0, 2, 3, 1, 4).reshape(B, H, 2, nb, Q_BLOCK, dk).transpose(3, 0, 1, 2, 4, 5)
    kpos = jnp.arange(S)
    scale = dk ** -0.5

    def one_block(args):
        qb, i = args
        qpos = i * Q_BLOCK + jnp.arange(Q_BLOCK)
        logits = jnp.einsum('bhmqd,bhmkd->bhmqk', qb, kh).astype(jnp.float32) * scale
        logits = jnp.where(qpos[:, None] >= kpos[None, :], logits, -jnp.inf)
        p = jax.nn.softmax(logits, axis=-1)
        w = p[:, :, 0] - lam * p[:, :, 1]
        return jnp.einsum('bhqk,bhkd->bhqd', w.astype(vh.dtype), vh)

    out = lax.map(one_block, (q_blocks, jnp.arange(nb)))
    out = out.transpose(1, 0, 3, 2, 4).reshape(B, S, H, dv)
    out = rms_norm(out, subln_g) * (1.0 - lambda_init)
    return out.reshape(B, S, H * dv)


def forgetting_attention(q, k, v, f_logit):
    B, S, H, d = q.shape
    nb = S // Q_BLOCK
    logf = jax.nn.log_sigmoid(f_logit.astype(jnp.float32))
    cum = jnp.cumsum(logf, axis=1).transpose(0, 2, 1)
    kh = k.transpose(0, 2, 1, 3)
    vh = v.transpose(0, 2, 1, 3)
    q_blocks = q.transpose(0, 2, 1, 3).reshape(B, H, nb, Q_BLOCK, d).transpose(2, 0, 1, 3, 4)
    c_blocks = cum.reshape(B, H, nb, Q_BLOCK).transpose(2, 0, 1, 3)
    kpos = jnp.arange(S)
    scale = d ** -0.5

    def one_block(args):
        qb, cq, i = args
        qpos = i * Q_BLOCK + jnp.arange(Q_BLOCK)
        logits = jnp.einsum('bhqd,bhkd->bhqk', qb, kh).astype(jnp.float32) * scale
        logits = logits + cq[..., None] - cum[:, :, None, :]
        logits = jnp.where(qpos[:, None] >= kpos[None, :], logits, -jnp.inf)
        p = jax.nn.softmax(logits, axis=-1)
        return jnp.einsum('bhqk,bhkd->bhqd', p.astype(vh.dtype), vh)

    out = lax.map(one_block, (q_blocks, c_blocks, jnp.arange(nb)))
    return out.transpose(1, 0, 3, 2, 4).reshape(B, S, H * d)


def multiscale_pool(h, w_pool, scale):
    B, S, _ = h.shape
    hg = h.reshape(B, S, POOL_GROUPS, POOL_GROUP_DIM)
    csum = jnp.cumsum(hg.astype(jnp.float32), axis=1)
    csum = jnp.concatenate([jnp.zeros_like(csum[:, :1]), csum], axis=1)
    win = jnp.array(POOL_WINDOWS, dtype=jnp.int32)
    t = jnp.arange(S, dtype=jnp.int32)[:, None]
    start = jnp.maximum(t + 1 - win[None, :], 0)
    gidx = jnp.arange(POOL_GROUPS)[None, :]
    total = csum[:, 1:] - csum[:, start, gidx]
    count = jnp.minimum(t + 1, win[None, :]).astype(jnp.float32)[None, :, :, None]
    pooled = (total / count - hg.astype(jnp.float32)).astype(h.dtype)
    y = jnp.einsum('bsgc,gcd->bsgd', pooled, w_pool)
    return y.reshape(B, S, POOL_GROUPS * POOL_GROUP_DIM) * scale


def setup_inputs(seed: int = 0) -> dict:
    key = jax.random.key(seed)
    ks = jax.random.split(key, 24)
    L = DEPTH
    f32 = jnp.float32

    def nrm(k, shape, s):
        return jax.random.normal(k, shape, f32) * s

    x = jax.random.normal(ks[0], (BATCH, SEQ, D_MODEL), f32)
    offs = jax.random.randint(ks[1], (BATCH, 1), 0, 4096, dtype=jnp.int32)
    positions = offs + jnp.arange(SEQ, dtype=jnp.int32)[None, :]
    return {
        "x": x,
        "positions": positions,
        "norm_mix_pre": 1.0 + nrm(ks[2], (L, D_MODEL), 0.05),
        "norm_mix_post": 1.0 + nrm(ks[3], (L, D_MODEL), 0.05),
        "norm_ffn_pre": 1.0 + nrm(ks[4], (L, D_MODEL), 0.05),
        "norm_ffn_post": 1.0 + nrm(ks[5], (L, D_MODEL), 0.05),
        "w_in": nrm(ks[6], (L, D_MODEL, IN_COLS), D_MODEL ** -0.5),
        "gm_ln_g": 1.0 + nrm(ks[7], (L, BRANCH_WIDTH), 0.05),
        "gm_ln_b": nrm(ks[8], (L, BRANCH_WIDTH), 0.02),
        "gm_w_s": nrm(ks[9], (L, GM_GROUPS, GM_CHUNK, GM_CHUNK), GM_CHUNK ** -0.5),
        "gm_b_s": 1.0 + nrm(ks[10], (L, GM_GROUPS, GM_CHUNK), 0.1),
        "da_lambda": nrm(ks[11], (L, 4, DA_QK_DIM), 0.1),
        "da_subln_g": 1.0 + nrm(ks[12], (L, DA_V_DIM), 0.05),
        "fa_b_f": 2.0 + nrm(ks[13], (L, FA_HEADS), 0.1),
        "pool_w": nrm(ks[14], (L, POOL_GROUPS, POOL_GROUP_DIM, POOL_GROUP_DIM), POOL_GROUP_DIM ** -0.5),
        "pool_scale": 1.0 + nrm(ks[15], (L, BRANCH_WIDTH), 0.1),
        "w_branch": nrm(ks[16], (L, N_BRANCH, BRANCH_WIDTH, D_MODEL), BRANCH_WIDTH ** -0.5),
        "w_out": nrm(ks[17], (L, D_MODEL, D_MODEL), D_MODEL ** -0.5),
        "w_ffn_up": nrm(ks[18], (L, D_MODEL, FFN_DIM), D_MODEL ** -0.5),
        "w_ffn_down": nrm(ks[19], (L, FFN_DIM, D_MODEL), FFN_DIM ** -0.5),
    }


def reference(x, positions, norm_mix_pre, norm_mix_post, norm_ffn_pre, norm_ffn_post,
              w_in, gm_ln_g, gm_ln_b, gm_w_s, gm_b_s, da_lambda, da_subln_g, fa_b_f,
              pool_w, pool_scale, w_branch, w_out, w_ffn_up, w_ffn_down):
    B, S, _ = x.shape
    cos, sin = rotary_tables(positions, DA_ROT_DIM)
    h = x
    for l in range(DEPTH):
        lambda_init = 0.8 - 0.6 * math.exp(-0.3 * l)
        xn = rms_norm(h, norm_mix_pre[l])
        proj = xn @ w_in[l]

        o_a = chunked_spatial_gating(proj[..., A_U:A_V], proj[..., A_V:B_Q],
                                     gm_ln_g[l], gm_ln_b[l], gm_w_s[l], gm_b_s[l])

        q_b = apply_partial_rotary(proj[..., B_Q:B_K].reshape(B, S, 2 * DA_HEADS, DA_QK_DIM), cos, sin)
        k_b = apply_partial_rotary(proj[..., B_K:B_V].reshape(B, S, 2 * DA_HEADS, DA_QK_DIM), cos, sin)
        v_b = proj[..., B_V:C_Q].reshape(B, S, DA_HEADS, DA_V_DIM)
        lp = da_lambda[l].astype(jnp.float32)
        lam = jnp.exp(jnp.sum(lp[0] * lp[1])) - jnp.exp(jnp.sum(lp[2] * lp[3])) + lambda_init
        o_b = differential_attention(q_b.reshape(B, S, DA_HEADS, 2, DA_QK_DIM),
                                     k_b.reshape(B, S, DA_HEADS, 2, DA_QK_DIM),
                                     v_b, lam, da_subln_g[l], lambda_init)

        q_c = proj[..., C_Q:C_K].reshape(B, S, FA_HEADS, FA_HEAD_DIM)
        k_c = proj[..., C_K:C_V].reshape(B, S, FA_HEADS, FA_HEAD_DIM)
        v_c = proj[..., C_V:C_F].reshape(B, S, FA_HEADS, FA_HEAD_DIM)
        f_logit = proj[..., C_F:D_H] + fa_b_f[l]
        o_c = forgetting_attention(q_c, k_c, v_c, f_logit)

        o_d = multiscale_pool(proj[..., D_H:GATE], pool_w[l], pool_scale[l])

        branches = jnp.stack([o_a, o_b, o_c, o_d], axis=2)
        gates = jax.nn.sigmoid(proj[..., GATE:].reshape(B, S, N_BRANCH, D_MODEL))
        branch_d = jnp.einsum('bsnc,ncd->bsnd', branches, w_branch[l])
        merged = jnp.einsum('bsnd,bsnd->bsd', gates, branch_d)
        h = h + rms_norm(merged @ w_out[l], norm_mix_post[l])

        hn = rms_norm(h, norm_ffn_pre[l])
        ff = jnp.square(jax.nn.relu(hn @ w_ffn_up[l])) @ w_ffn_down[l]
        h = h + rms_norm(ff, norm_ffn_post[l])
    return h
```

```python
import functools
import math

import jax
import jax.numpy as jnp
from jax import lax
from jax.experimental import pallas as pl
from jax.experimental.pallas import tpu as pltpu

F32 = jnp.float32
BF16 = jnp.bfloat16

DEPTH = 2
N_BRANCH = 4
GM_GROUPS = 4
GM_CHUNK = 128
DA_HEADS = 4
DA_QK_DIM = 64
DA_ROT_DIM = 16
FA_HEADS = 4
FA_HEAD_DIM = 128
POOL_WINDOWS = (2, 4, 8, 16)
ROPE_THETA = 500000.0
NORM_EPS = 1e-6
MASK_VALUE = -1e30

LANES = 128
HEAD_COLS = 128
PROJ_TN = 512

J_U, J_V, J_BQ, J_BK, J_BV, J_CQ, J_CK, J_CV, J_DH = range(16, 25)


def _rms(x, g):
    return x * lax.rsqrt(jnp.mean(x * x, axis=-1, keepdims=True) + NORM_EPS) * g


def _cparams(sem, vmem_mb=None):
    kw = dict(dimension_semantics=sem)
    if vmem_mb is not None:
        kw["vmem_limit_bytes"] = vmem_mb * 1024 * 1024
    return pltpu.CompilerParams(**kw)


def _rope_kernel(pos_ref, inv_ref, c_ref, sa_ref, sb_ref):
    ang = pos_ref[...].astype(F32) * inv_ref[...]
    sub = lax.broadcasted_iota(jnp.int32, ang.shape, 1) % DA_QK_DIM
    half = DA_ROT_DIM // 2
    s = jnp.sin(ang)
    c_ref[...] = jnp.cos(ang)
    sa_ref[...] = jnp.where(sub < half, -s, 0.0)
    sb_ref[...] = jnp.where((sub >= half) & (sub < 2 * half), s, 0.0)


def _rope_tables(positions):
    T = positions.size
    tm = 1024
    half = DA_ROT_DIM // 2
    inv = 1.0 / (ROPE_THETA ** (jnp.arange(0, DA_ROT_DIM, 2, dtype=F32) / DA_ROT_DIM))
    sub = jnp.arange(LANES) % DA_QK_DIM
    inv_lane = jnp.where(sub < 2 * half, inv[sub % half], 0.0).reshape(1, LANES)
    tab = jax.ShapeDtypeStruct((T, LANES), F32)
    return pl.pallas_call(
        _rope_kernel,
        grid=(T // tm,),
        in_specs=[pl.BlockSpec((tm, 1), lambda i: (i, 0)),
                  pl.BlockSpec((1, LANES), lambda i: (0, 0))],
        out_specs=[pl.BlockSpec((tm, LANES), lambda i: (i, 0))] * 3,
        out_shape=[tab] * 3,
        compiler_params=_cparams(("parallel",)),
        name="rope_tables",
    )(positions.reshape(T, 1), inv_lane)


def _rotate(x, c, sa, sb):
    outs = []
    for g in range(x.shape[1] // LANES):
        xg = x[:, g * LANES:(g + 1) * LANES]
        outs.append(xg * c + pltpu.roll(xg, LANES - DA_ROT_DIM // 2, 1) * sa
                    + pltpu.roll(xg, DA_ROT_DIM // 2, 1) * sb)
    return jnp.concatenate(outs, axis=1)


def _inproj_kernel(h_ref, g_ref, w_ref, wf_ref, c_ref, sa_ref, sb_ref, out_ref, f_ref, xn_ref):
    j = pl.program_id(1)

    @pl.when(j == 0)
    def _():
        xn_ref[...] = _rms(h_ref[...], g_ref[...]).astype(BF16)
        f_ref[...] = jnp.dot(xn_ref[...], wf_ref[...], preferred_element_type=F32)

    acc = jnp.dot(xn_ref[...], w_ref[...], preferred_element_type=F32)
    is_bq = j == J_BQ
    is_bk = j == J_BK
    is_cq = j == J_CQ

    @pl.when(is_bq)
    def _():
        r = _rotate(acc, c_ref[...], sa_ref[...], sb_ref[...])
        out_ref[...] = (r * (DA_QK_DIM ** -0.5)).astype(out_ref.dtype)

    @pl.when(is_bk)
    def _():
        out_ref[...] = _rotate(acc, c_ref[...], sa_ref[...], sb_ref[...]).astype(out_ref.dtype)

    @pl.when(is_cq)
    def _():
        out_ref[...] = (acc * (FA_HEAD_DIM ** -0.5)).astype(out_ref.dtype)

    @pl.when(jnp.logical_not(is_bq | is_bk | is_cq))
    def _():
        out_ref[...] = acc.astype(out_ref.dtype)


def _inproj(h, g, w, wf, tables, tm=512):
    T, D = h.shape
    N = w.shape[1]
    tn = PROJ_TN
    row = lambda i, j: (i, 0)
    return pl.pallas_call(
        _inproj_kernel,
        grid=(T // tm, N // tn),
        in_specs=[pl.BlockSpec((tm, D), row),
                  pl.BlockSpec((1, D), lambda i, j: (0, 0)),
                  pl.BlockSpec((D, tn), lambda i, j: (0, j)),
                  pl.BlockSpec((D, LANES), lambda i, j: (0, 0)),
                  pl.BlockSpec((tm, LANES), row),
                  pl.BlockSpec((tm, LANES), row),
                  pl.BlockSpec((tm, LANES), row)],
        out_specs=[pl.BlockSpec((tm, tn), lambda i, j: (i, j)),
                   pl.BlockSpec((tm, LANES), row)],
        out_shape=[jax.ShapeDtypeStruct((T, N), BF16),
                   jax.ShapeDtypeStruct((T, LANES), F32)],
        scratch_shapes=[pltpu.VMEM((tm, D), BF16)],
        compiler_params=_cparams(("parallel", "arbitrary"), 40),
        name="inproj",
    )(h, g, w, wf, *tables)


def _gmlp_kernel(u_ref, v_ref, lng_ref, lnb_ref, ws_ref, bst_ref, o_ref):
    v = v_ref[...].astype(F32)
    mu = jnp.mean(v, axis=-1, keepdims=True)
    vc = v - mu
    var = jnp.mean(vc * vc, axis=-1, keepdims=True)
    vn = (vc * lax.rsqrt(var + NORM_EPS) * lng_ref[...] + lnb_ref[...]).astype(BF16)
    r = lax.broadcasted_iota(jnp.int32, (GM_CHUNK, GM_CHUNK), 0)
    c = lax.broadcasted_iota(jnp.int32, (GM_CHUNK, GM_CHUNK), 1)
    causal = r >= c
    for g in range(GM_GROUPS):
        wg = jnp.where(causal, ws_ref[g], 0.0).astype(BF16)
        bcol = bst_ref[:, g:g + 1]
        cols = slice(g * HEAD_COLS, (g + 1) * HEAD_COLS)
        for n in range(v.shape[0] // GM_CHUNK):
            rows = slice(n * GM_CHUNK, (n + 1) * GM_CHUNK)
            mixed = jnp.dot(wg, vn[rows, cols], preferred_element_type=F32) + bcol
            o_ref[rows, cols] = (u_ref[rows, cols].astype(F32) * mixed).astype(o_ref.dtype)


def _gmlp(proj, ln_g, ln_b, w_s, b_s_t, tm=512):
    T = proj.shape[0]
    W = GM_GROUPS * HEAD_COLS
    const2 = lambda i: (0, 0)
    return pl.pallas_call(
        _gmlp_kernel,
        grid=(T // tm,),
        in_specs=[pl.BlockSpec((tm, W), lambda i: (i, J_U)),
                  pl.BlockSpec((tm, W), lambda i: (i, J_V)),
                  pl.BlockSpec((1, W), const2),
                  pl.BlockSpec((1, W), const2),
                  pl.BlockSpec((GM_GROUPS, GM_CHUNK, GM_CHUNK), lambda i: (0, 0, 0)),
                  pl.BlockSpec((GM_CHUNK, GM_GROUPS), const2)],
        out_specs=pl.BlockSpec((tm, W), lambda i: (i, 0)),
        out_shape=jax.ShapeDtypeStruct((T, W), BF16),
        compiler_params=_cparams(("parallel",)),
        name="gmlp",
    )(proj, proj, ln_g, ln_b, w_s, b_s_t)


def _softmax_step(s, carry, v):
    m, l, acc = carry
    m_new = jnp.maximum(m, jnp.max(s, axis=-1, keepdims=True))
    alpha = jnp.exp(m - m_new)
    p = jnp.exp(s - m_new)
    l = alpha * l + jnp.sum(p, axis=-1, keepdims=True)
    acc = alpha * acc + jnp.dot(p.astype(v.dtype), v, preferred_element_type=F32)
    return m_new, l, acc


def _softmax_init(tq, dv):
    return (jnp.full((tq, 1), MASK_VALUE, F32), jnp.zeros((tq, 1), F32), jnp.zeros((tq, dv), F32))


def _qk(q, k):
    return lax.dot_general(q, k, (((1,), (1,)), ((), ())), preferred_element_type=F32)


def _causal(tq):
    r = lax.broadcasted_iota(jnp.int32, (tq, tq), 0)
    c = lax.broadcasted_iota(jnp.int32, (tq, tq), 1)
    return r >= c


def _diff_attn_kernel(q_ref, k_ref, v_ref, lam_ref, g_ref, o_ref, *, tq, lambda_init):
    i = pl.program_id(2)
    qf = q_ref[...].astype(F32)
    lane = lax.broadcasted_iota(jnp.int32, qf.shape, 1)
    qs = (jnp.where(lane < DA_QK_DIM, qf, 0.0).astype(BF16),
          jnp.where(lane >= DA_QK_DIM, qf, 0.0).astype(BF16))
    causal = _causal(tq)

    def chunk(kc, carry, masked):
        ks = pl.multiple_of(kc * tq, tq)
        k = k_ref[pl.ds(ks, tq), :]
        v = v_ref[pl.ds(ks, tq), :]
        out = []
        for qm, cm in zip(qs, carry):
            s = _qk(qm, k)
            if masked:
                s = jnp.where(causal, s, MASK_VALUE)
            out.append(_softmax_step(s, cm, v))
        return tuple(out)

    init = (_softmax_init(tq, HEAD_COLS), _softmax_init(tq, HEAD_COLS))
    carry = lax.fori_loop(0, i, lambda kc, c: chunk(kc, c, False), init)
    (_, l0, a0), (_, l1, a1) = chunk(i, carry, True)

    lp = lam_ref[...]

    def total(x):
        return jnp.sum(jnp.sum(x, axis=1, keepdims=True), axis=0, keepdims=True)

    lam = jnp.exp(total(lp[0:1] * lp[1:2])) - jnp.exp(total(lp[2:3] * lp[3:4])) + lambda_init
    o = a0 / l0 - lam * (a1 / l1)
    o_ref[...] = (_rms(o, g_ref[...]) * (1.0 - lambda_init)).astype(o_ref.dtype)


def _diff_attn(proj, lam, subln_g, B, S, lambda_init, tq=256):
    T = proj.shape[0]
    nq = S // tq
    per_tile = PROJ_TN // HEAD_COLS
    kern = functools.partial(_diff_attn_kernel, tq=tq, lambda_init=lambda_init)
    return pl.pallas_call(
        kern,
        grid=(B, DA_HEADS, nq),
        in_specs=[pl.BlockSpec((tq, HEAD_COLS), lambda b, h, i: (b * nq + i, J_BQ * per_tile + h)),
                  pl.BlockSpec((S, HEAD_COLS), lambda b, h, i: (b, J_BK * per_tile + h)),
                  pl.BlockSpec((S, HEAD_COLS), lambda b, h, i: (b, J_BV * per_tile + h)),
                  pl.BlockSpec(lam.shape, lambda b, h, i: (0, 0)),
                  pl.BlockSpec((1, HEAD_COLS), lambda b, h, i: (0, 0))],
        out_specs=pl.BlockSpec((tq, HEAD_COLS), lambda b, h, i: (b * nq + i, h)),
        out_shape=jax.ShapeDtypeStruct((T, DA_HEADS * HEAD_COLS), BF16),
        compiler_params=_cparams(("parallel", "parallel", "arbitrary")),
        name="diff_attn",
    )(proj, proj, proj, lam, subln_g)


def _fcum_kernel(f_ref, bf_ref, cum_ref, cumt_ref, *, blk):
    x = f_ref[...] + bf_ref[...]
    logf = jnp.minimum(x, 0.0) - jnp.log1p(jnp.exp(-jnp.abs(x)))
    r = lax.broadcasted_iota(jnp.int32, (blk, blk), 0)
    c = lax.broadcasted_iota(jnp.int32, (blk, blk), 1)
    tri = (r >= c).astype(F32)
    carry = jnp.zeros((1, LANES), F32)
    for n in range(x.shape[0] // blk):
        part = jnp.dot(tri, logf[n * blk:(n + 1) * blk], preferred_element_type=F32,
                       precision=lax.Precision.HIGHEST) + carry
        cum_ref[n * blk:(n + 1) * blk, :] = part
        carry = part[blk - 1:blk, :]
    cum_t = cum_ref[...].T
    for hh in range(FA_HEADS):
        cumt_ref[hh] = cum_t[hh:hh + 1, :]


def _fcum(f_logit, b_f, B, S):
    T = f_logit.shape[0]
    return pl.pallas_call(
        functools.partial(_fcum_kernel, blk=256),
        grid=(B,),
        in_specs=[pl.BlockSpec((S, LANES), lambda b: (b, 0)),
                  pl.BlockSpec((1, LANES), lambda b: (0, 0))],
        out_specs=[pl.BlockSpec((S, LANES), lambda b: (b, 0)),
                   pl.BlockSpec((FA_HEADS, 1, S), lambda b: (b, 0, 0))],
        out_shape=[jax.ShapeDtypeStruct((T, LANES), F32),
                   jax.ShapeDtypeStruct((B * FA_HEADS, 1, S), F32)],
        compiler_params=_cparams(("parallel",)),
        name="forget_cumsum",
    )(f_logit, b_f)


def _fox_attn_kernel(q_ref, k_ref, v_ref, cum_ref, cumt_ref, o_ref, *, tq):
    h = pl.program_id(1)
    i = pl.program_id(2)
    q = q_ref[...]
    cum = cum_ref[...]
    lane = lax.broadcasted_iota(jnp.int32, cum.shape, 1)
    cq = jnp.sum(jnp.where(lane == h, cum, 0.0), axis=-1, keepdims=True)
    causal = _causal(tq)

    def chunk(kc, carry, masked):
        ks = pl.multiple_of(kc * tq, tq)
        k = k_ref[pl.ds(ks, tq), :]
        v = v_ref[pl.ds(ks, tq), :]
        ck = cumt_ref[0, :, pl.ds(ks, tq)]
        s = _qk(q, k) + (cq - ck)
        if masked:
            s = jnp.where(causal, s, MASK_VALUE)
        return _softmax_step(s, carry, v)

    carry = lax.fori_loop(0, i, lambda kc, c: chunk(kc, c, False), _softmax_init(tq, HEAD_COLS))
    _, l, acc = chunk(i, carry, True)
    o_ref[...] = (acc / l).astype(o_ref.dtype)


def _fox_attn(proj, cum, cum_t, B, S, tq=256):
    T = proj.shape[0]
    nq = S // tq
    per_tile = PROJ_TN // HEAD_COLS
    return pl.pallas_call(
        functools.partial(_fox_attn_kernel, tq=tq),
        grid=(B, FA_HEADS, nq),
        in_specs=[pl.BlockSpec((tq, HEAD_COLS), lambda b, h, i: (b * nq + i, J_CQ * per_tile + h)),
                  pl.BlockSpec((S, HEAD_COLS), lambda b, h, i: (b, J_CK * per_tile + h)),
                  pl.BlockSpec((S, HEAD_COLS), lambda b, h, i: (b, J_CV * per_tile + h)),
                  pl.BlockSpec((tq, LANES), lambda b, h, i: (b * nq + i, 0)),
                  pl.BlockSpec((1, 1, S), lambda b, h, i: (b * FA_HEADS + h, 0, 0))],
        out_specs=pl.BlockSpec((tq, HEAD_COLS), lambda b, h, i: (b * nq + i, h)),
        out_shape=jax.ShapeDtypeStruct((T, FA_HEADS * HEAD_COLS), BF16),
        compiler_params=_cparams(("parallel", "parallel", "arbitrary")),
        name="fox_attn",
    )(proj, proj, proj, cum, cum_t)


def _pool_kernel(h_ref, w_ref, sc_ref, o_ref):
    S = h_ref.shape[0]
    t = lax.broadcasted_iota(jnp.int32, (S, HEAD_COLS), 0)
    for g, win in enumerate(POOL_WINDOWS):
        cols = slice(g * HEAD_COLS, (g + 1) * HEAD_COLS)
        x = h_ref[:, cols].astype(F32)
        tot = x
        span = 1
        while span < win:
            tot = tot + jnp.where(t >= span, pltpu.roll(tot, span, 0), 0.0)
            span *= 2
        cnt = jnp.minimum(t + 1, win).astype(F32)
        pooled = (tot / cnt - x).astype(BF16)
        y = jnp.dot(pooled, w_ref[g].astype(BF16), preferred_element_type=F32)
        o_ref[:, cols] = (y * sc_ref[:, cols]).astype(o_ref.dtype)


def _pool(proj, w_pool, scale, B, S):
    T = proj.shape[0]
    W = len(POOL_WINDOWS) * HEAD_COLS
    return pl.pallas_call(
        _pool_kernel,
        grid=(B,),
        in_specs=[pl.BlockSpec((S, W), lambda b: (b, J_DH)),
                  pl.BlockSpec(w_pool.shape, lambda b: (0, 0, 0)),
                  pl.BlockSpec((1, W), lambda b: (0, 0))],
        out_specs=pl.BlockSpec((S, W), lambda b: (b, 0)),
        out_shape=jax.ShapeDtypeStruct((T, W), BF16),
        compiler_params=_cparams(("parallel",)),
        name="pool",
    )(proj, w_pool, scale)


def _merge_kernel(oa_ref, ob_ref, oc_ref, od_ref, gate_ref, wb_ref, wo_ref, g_ref, h_ref, out_ref):
    D = h_ref.shape[1]
    merged = None
    for n, br in enumerate((oa_ref, ob_ref, oc_ref, od_ref)):
        bd = jnp.dot(br[...], wb_ref[n], preferred_element_type=F32)
        term = jax.nn.sigmoid(gate_ref[:, n * D:(n + 1) * D].astype(F32)) * bd
        merged = term if merged is None else merged + term
    y = jnp.dot(merged.astype(BF16), wo_ref[...], preferred_element_type=F32)
    out_ref[...] = h_ref[...] + _rms(y, g_ref[...])


def _merge(branches, proj, w_branch, w_out, g, h, tm=256):
    T, D = h.shape
    W = branches[0].shape[1]
    row = lambda i: (i, 0)
    once = pl.Buffered(1)
    return pl.pallas_call(
        _merge_kernel,
        grid=(T // tm,),
        in_specs=[pl.BlockSpec((tm, W), row)] * N_BRANCH + [
            pl.BlockSpec((tm, N_BRANCH * D), row),
            pl.BlockSpec(w_branch.shape, lambda i: (0, 0, 0), pipeline_mode=once),
            pl.BlockSpec(w_out.shape, lambda i: (0, 0), pipeline_mode=once),
            pl.BlockSpec((1, D), lambda i: (0, 0)),
            pl.BlockSpec((tm, D), row)],
        out_specs=pl.BlockSpec((tm, D), row),
        out_shape=jax.ShapeDtypeStruct((T, D), F32),
        compiler_params=_cparams(("parallel",), 48),
        name="merge_out",
    )(*branches, proj, w_branch, w_out, g, h)


def _ffn_kernel(h_ref, g1_ref, wu_ref, wd_ref, g2_ref, out_ref, hn_ref, acc_ref):
    f = pl.program_id(1)

    @pl.when(f == 0)
    def _():
        hn_ref[...] = _rms(h_ref[...], g1_ref[...]).astype(BF16)
        acc_ref[...] = jnp.zeros_like(acc_ref)

    up = jnp.dot(hn_ref[...], wu_ref[...], preferred_element_type=F32)
    a = jnp.square(jnp.maximum(up, 0.0)).astype(BF16)
    acc_ref[...] += jnp.dot(a, wd_ref[...], preferred_element_type=F32)

    @pl.when(f == pl.num_programs(1) - 1)
    def _():
        out_ref[...] = h_ref[...] + _rms(acc_ref[...], g2_ref[...])


def _ffn(h, g1, w_up, w_down, g2, tm=512, tf=512):
    T, D = h.shape
    F = w_up.shape[1]
    row = lambda i, f: (i, 0)
    vec = lambda i, f: (0, 0)
    return pl.pallas_call(
        _ffn_kernel,
        grid=(T // tm, F // tf),
        in_specs=[pl.BlockSpec((tm, D), row),
                  pl.BlockSpec((1, D), vec),
                  pl.BlockSpec((D, tf), lambda i, f: (0, f)),
                  pl.BlockSpec((tf, D), lambda i, f: (f, 0)),
                  pl.BlockSpec((1, D), vec)],
        out_specs=pl.BlockSpec((tm, D), row),
        out_shape=jax.ShapeDtypeStruct((T, D), F32),
        scratch_shapes=[pltpu.VMEM((tm, D), BF16), pltpu.VMEM((tm, D), F32)],
        compiler_params=_cparams(("parallel", "arbitrary"), 48),
        name="ffn",
    )(h, g1, w_up, w_down, g2)


def _prep_w_in(w, D):
    n_main = 8 * PROJ_TN
    f0 = n_main
    d0 = f0 + FA_HEADS
    g0 = d0 + PROJ_TN
    w_main = jnp.concatenate([w[:, g0:], w[:, :n_main], w[:, d0:g0]], axis=1).astype(BF16)
    w_f = jnp.pad(w[:, f0:d0], ((0, 0), (0, LANES - FA_HEADS))).astype(BF16)
    return w_main, w_f


def kernel(x, positions, norm_mix_pre, norm_mix_post, norm_ffn_pre, norm_ffn_post, w_in, gm_ln_g,
           gm_ln_b, gm_w_s, gm_b_s, da_lambda, da_subln_g, fa_b_f, pool_w, pool_scale, w_branch,
           w_out, w_ffn_up, w_ffn_down):
    B, S, D = x.shape
    T = B * S
    h = x.reshape(T, D)
    tables = _rope_tables(positions)
    row = lambda a: a.reshape(1, -1)
    for l in range(DEPTH):
        lambda_init = 0.8 - 0.6 * math.exp(-0.3 * l)
        w_main, w_f = _prep_w_in(w_in[l], D)
        proj, f_logit = _inproj(h, row(norm_mix_pre[l]), w_main, w_f, tables)

        o_a = _gmlp(proj, row(gm_ln_g[l]), row(gm_ln_b[l]), gm_w_s[l], gm_b_s[l].T)
        o_b = _diff_attn(proj, da_lambda[l], row(da_subln_g[l]), B, S, lambda_init)
        b_f = jnp.pad(fa_b_f[l], (0, LANES - FA_HEADS)).reshape(1, LANES)
        cum, cum_t = _fcum(f_logit, b_f, B, S)
        o_c = _fox_attn(proj, cum, cum_t, B, S)
        o_d = _pool(proj, pool_w[l], row(pool_scale[l]), B, S)

        h = _merge((o_a, o_b, o_c, o_d), proj, w_branch[l].astype(BF16), w_out[l].astype(BF16),
                   row(norm_mix_post[l]), h)
        h = _ffn(h, row(norm_ffn_pre[l]), w_ffn_up[l].astype(BF16), w_ffn_down[l].astype(BF16),
                 row(norm_ffn_post[l]))
    return h.reshape(B, S, D)
```

```python
import functools
import math

import jax
import jax.numpy as jnp
from jax import lax
from jax.experimental import pallas as pl
from jax.experimental.pallas import tpu as pltpu

F32 = jnp.float32
BF16 = jnp.bfloat16

DEPTH = 2
N_BRANCH = 4
GM_GROUPS = 4
GM_CHUNK = 128
DA_HEADS = 4
DA_QK_DIM = 64
DA_ROT_DIM = 16
FA_HEADS = 4
FA_HEAD_DIM = 128
POOL_WINDOWS = (2, 4, 8, 16)
ROPE_THETA = 500000.0
NORM_EPS = 1e-6
MASK_VALUE = -1e30

LANES = 128
HEAD_COLS = 128
SLOT = 512

N_GATE_SLOTS = 16
J_U, J_V, J_BQ, J_BK, J_BV, J_CQ, J_CK, J_CV, J_DH = range(N_GATE_SLOTS, N_GATE_SLOTS + 9)
N_SLOTS = J_DH + 1
N_ALIGNED = 8


def _rms(x, g):
    return x * lax.rsqrt(jnp.mean(x * x, axis=-1, keepdims=True) + NORM_EPS) * g


def _cparams(sem, vmem_mb=None):
    kw = dict(dimension_semantics=sem)
    if vmem_mb is not None:
        kw["vmem_limit_bytes"] = vmem_mb * 1024 * 1024
    return pltpu.CompilerParams(**kw)


def _rope_kernel(pos_ref, inv_ref, c_ref, sa_ref, sb_ref):
    ang = pos_ref[...].astype(F32) * inv_ref[...]
    sub = lax.broadcasted_iota(jnp.int32, ang.shape, 1) % DA_QK_DIM
    half = DA_ROT_DIM // 2
    s = jnp.sin(ang)
    c_ref[...] = jnp.cos(ang)
    sa_ref[...] = jnp.where(sub < half, -s, 0.0)
    sb_ref[...] = jnp.where((sub >= half) & (sub < 2 * half), s, 0.0)


def _rope_tables(positions):
    T = positions.size
    tm = 1024
    half = DA_ROT_DIM // 2
    inv = 1.0 / (ROPE_THETA ** (jnp.arange(0, DA_ROT_DIM, 2, dtype=F32) / DA_ROT_DIM))
    sub = jnp.arange(LANES) % DA_QK_DIM
    inv_lane = jnp.where(sub < 2 * half, inv[sub % half], 0.0).reshape(1, LANES)
    tab = jax.ShapeDtypeStruct((T, LANES), F32)
    return pl.pallas_call(
        _rope_kernel,
        grid=(T // tm,),
        in_specs=[pl.BlockSpec((tm, 1), lambda i: (i, 0)),
                  pl.BlockSpec((1, LANES), lambda i: (0, 0))],
        out_specs=[pl.BlockSpec((tm, LANES), lambda i: (i, 0))] * 3,
        out_shape=[tab] * 3,
        compiler_params=_cparams(("parallel",)),
        name="rope_tables",
    )(positions.reshape(T, 1), inv_lane)


def _rotate(x, c, sa, sb):
    half = DA_ROT_DIM // 2
    return x * c + pltpu.roll(x, LANES - half, 1) * sa + pltpu.roll(x, half, 1) * sb


def _wprep_kernel(a_ref, b_ref, o_ref):
    j = pl.program_id(0)
    shifted = (j < N_GATE_SLOTS) | (j == J_DH)

    @pl.when(jnp.logical_not(shifted))
    def _():
        o_ref[...] = a_ref[...].astype(o_ref.dtype)

    @pl.when(shifted)
    def _():
        lane = lax.broadcasted_iota(jnp.int32, (a_ref.shape[0], LANES), 1)
        keep = lane < LANES - FA_HEADS
        n = SLOT // LANES
        rolled = [pltpu.roll(a_ref[:, g * LANES:(g + 1) * LANES], LANES - FA_HEADS, 1) for g in range(n)]
        rolled.append(pltpu.roll(b_ref[...], LANES - FA_HEADS, 1))
        for g in range(n):
            o_ref[:, g * LANES:(g + 1) * LANES] = jnp.where(keep, rolled[g], rolled[g + 1]).astype(o_ref.dtype)


def _src_slot(j):
    return jnp.where(j < N_GATE_SLOTS, j + N_ALIGNED + 1, jnp.where(j == J_DH, N_ALIGNED, j - N_GATE_SLOTS))


def _prep_w_in(w_in, l, tr=1024):
    _, D, _ = w_in.shape
    per = SLOT // LANES
    return pl.pallas_call(
        _wprep_kernel,
        grid=(N_SLOTS, D // tr),
        in_specs=[pl.BlockSpec((None, tr, SLOT), lambda j, r: (l, r, _src_slot(j))),
                  pl.BlockSpec((None, tr, LANES), lambda j, r: (l, r, per * (_src_slot(j) + 1)))],
        out_specs=pl.BlockSpec((tr, SLOT), lambda j, r: (r, j)),
        out_shape=jax.ShapeDtypeStruct((D, N_SLOTS * SLOT), BF16),
        compiler_params=_cparams(("parallel", "parallel")),
        name="w_in_prep",
    )(w_in, w_in)


def _inproj_kernel(h_ref, g_ref, w_ref, wf_ref, out_ref, f_ref, xn_ref):
    @pl.when(pl.program_id(1) == 0)
    def _():
        xn_ref[...] = _rms(h_ref[...], g_ref[...]).astype(BF16)
        f_ref[...] = jnp.dot(xn_ref[...], wf_ref[...], preferred_element_type=F32)

    out_ref[...] = jnp.dot(xn_ref[...], w_ref[...], preferred_element_type=F32).astype(out_ref.dtype)


def _inproj(h, g, w, w_f, tm=1024):
    T, D = h.shape
    row = lambda i, j: (i, 0)
    return pl.pallas_call(
        _inproj_kernel,
        grid=(T // tm, N_SLOTS),
        in_specs=[pl.BlockSpec((tm, D), row),
                  pl.BlockSpec((1, D), lambda i, j: (0, 0)),
                  pl.BlockSpec((D, SLOT), lambda i, j: (0, j)),
                  pl.BlockSpec((D, LANES), lambda i, j: (0, 0))],
        out_specs=[pl.BlockSpec((tm, SLOT), lambda i, j: (i, j)),
                   pl.BlockSpec((tm, LANES), row)],
        out_shape=[jax.ShapeDtypeStruct((T, N_SLOTS * SLOT), BF16),
                   jax.ShapeDtypeStruct((T, LANES), F32)],
        scratch_shapes=[pltpu.VMEM((tm, D), BF16)],
        compiler_params=_cparams(("parallel", "arbitrary"), 48),
        name="inproj",
    )(h, g, w, w_f)


def _gmlp_kernel(u_ref, v_ref, lng_ref, lnb_ref, ws_ref, bst_ref, o_ref):
    v = v_ref[...].astype(F32)
    mu = jnp.mean(v, axis=-1, keepdims=True)
    vc = v - mu
    var = jnp.mean(vc * vc, axis=-1, keepdims=True)
    vn = (vc * lax.rsqrt(var + NORM_EPS) * lng_ref[...] + lnb_ref[...]).astype(BF16)
    r = lax.broadcasted_iota(jnp.int32, (GM_CHUNK, GM_CHUNK), 0)
    c = lax.broadcasted_iota(jnp.int32, (GM_CHUNK, GM_CHUNK), 1)
    causal = r >= c
    for g in range(GM_GROUPS):
        wg = jnp.where(causal, ws_ref[g], 0.0).astype(BF16)
        bcol = bst_ref[:, g:g + 1]
        cols = slice(g * HEAD_COLS, (g + 1) * HEAD_COLS)
        for n in range(v.shape[0] // GM_CHUNK):
            rows = slice(n * GM_CHUNK, (n + 1) * GM_CHUNK)
            mixed = jnp.dot(wg, vn[rows, cols], preferred_element_type=F32) + bcol
            o_ref[rows, cols] = (u_ref[rows, cols].astype(F32) * mixed).astype(o_ref.dtype)


def _gmlp(proj, ln_g, ln_b, w_s, b_s_t, tm=512):
    T = proj.shape[0]
    const2 = lambda i: (0, 0)
    return pl.pallas_call(
        _gmlp_kernel,
        grid=(T // tm,),
        in_specs=[pl.BlockSpec((tm, SLOT), lambda i: (i, J_U)),
                  pl.BlockSpec((tm, SLOT), lambda i: (i, J_V)),
                  pl.BlockSpec((1, SLOT), const2),
                  pl.BlockSpec((1, SLOT), const2),
                  pl.BlockSpec((GM_GROUPS, GM_CHUNK, GM_CHUNK), lambda i: (0, 0, 0)),
                  pl.BlockSpec((GM_CHUNK, GM_GROUPS), const2)],
        out_specs=pl.BlockSpec((tm, SLOT), lambda i: (i, 0)),
        out_shape=jax.ShapeDtypeStruct((T, SLOT), BF16),
        compiler_params=_cparams(("parallel",)),
        name="gmlp",
    )(proj, proj, ln_g, ln_b, w_s, b_s_t)


def _qk(q, k):
    return lax.dot_general(q, k, (((1,), (1,)), ((), ())), preferred_element_type=F32)


def _causal(rows, tq):
    r = lax.broadcasted_iota(jnp.int32, (rows, tq), 0) % tq
    c = lax.broadcasted_iota(jnp.int32, (rows, tq), 1)
    return r >= c


def _row_softmax(s_off, s_diag):
    m = jnp.max(s_diag, axis=-1, keepdims=True)
    if s_off is not None:
        m = jnp.maximum(m, jnp.max(s_off, axis=-1, keepdims=True))
    p_diag = jnp.exp(s_diag - m)
    l = jnp.sum(p_diag, axis=-1, keepdims=True)
    p_off = None
    if s_off is not None:
        p_off = jnp.exp(s_off - m)
        l = l + jnp.sum(p_off, axis=-1, keepdims=True)
    return p_off, p_diag, l


def _diff_attn_kernel(q_ref, k_ref, v_ref, c_ref, sa_ref, sb_ref, lam_ref, g_ref, o_ref,
                      qs_ref, kr_ref, *, tq, lambda_init):
    S = q_ref.shape[0]
    c, sa, sb = c_ref[...], sa_ref[...], sb_ref[...]
    qf = _rotate(q_ref[...].astype(F32), c, sa, sb) * (DA_QK_DIM ** -0.5)
    lane = lax.broadcasted_iota(jnp.int32, qf.shape, 1)
    qs_ref[0] = jnp.where(lane < DA_QK_DIM, qf, 0.0).astype(BF16)
    qs_ref[1] = jnp.where(lane >= DA_QK_DIM, qf, 0.0).astype(BF16)
    kr_ref[...] = _rotate(k_ref[...].astype(F32), c, sa, sb).astype(BF16)

    lp = lam_ref[...]

    def total(x):
        return jnp.sum(jnp.sum(x, axis=1, keepdims=True), axis=0, keepdims=True)

    lam = jnp.exp(total(lp[0:1] * lp[1:2])) - jnp.exp(total(lp[2:3] * lp[3:4])) + lambda_init
    causal = _causal(2 * tq, tq)
    g = g_ref[...]

    def combine(p, r):
        return (p[:tq] * r[:tq] - p[tq:] * (lam * r[tq:])).astype(BF16)

    for i in range(S // tq):
        rows = pl.ds(i * tq, tq)
        q2 = jnp.concatenate([qs_ref[0, rows, :], qs_ref[1, rows, :]], axis=0)
        s_diag = jnp.where(causal, _qk(q2, kr_ref[rows, :]), MASK_VALUE)
        s_off = _qk(q2, kr_ref[pl.ds(0, i * tq), :]) if i else None
        p_off, p_diag, l = _row_softmax(s_off, s_diag)
        r = 1.0 / l
        o = jnp.dot(combine(p_diag, r), v_ref[rows, :], preferred_element_type=F32)
        if i:
            o = o + jnp.dot(combine(p_off, r), v_ref[pl.ds(0, i * tq), :], preferred_element_type=F32)
        o_ref[rows, :] = (_rms(o, g) * (1.0 - lambda_init)).astype(o_ref.dtype)


def _diff_attn(proj, tables, lam, subln_g, B, S, lambda_init, tq=256):
    T = proj.shape[0]
    per_slot = SLOT // HEAD_COLS
    seq = lambda b, h: (b, 0)
    kern = functools.partial(_diff_attn_kernel, tq=tq, lambda_init=lambda_init)
    return pl.pallas_call(
        kern,
        grid=(B, DA_HEADS),
        in_specs=[pl.BlockSpec((S, HEAD_COLS), lambda b, h: (b, J_BQ * per_slot + h)),
                  pl.BlockSpec((S, HEAD_COLS), lambda b, h: (b, J_BK * per_slot + h)),
                  pl.BlockSpec((S, HEAD_COLS), lambda b, h: (b, J_BV * per_slot + h)),
                  pl.BlockSpec((S, LANES), seq),
                  pl.BlockSpec((S, LANES), seq),
                  pl.BlockSpec((S, LANES), seq),
                  pl.BlockSpec(lam.shape, lambda b, h: (0, 0)),
                  pl.BlockSpec((1, HEAD_COLS), lambda b, h: (0, 0))],
        out_specs=pl.BlockSpec((S, HEAD_COLS), lambda b, h: (b, h)),
        out_shape=jax.ShapeDtypeStruct((T, DA_HEADS * HEAD_COLS), BF16),
        scratch_shapes=[pltpu.VMEM((2, S, HEAD_COLS), BF16), pltpu.VMEM((S, HEAD_COLS), BF16)],
        compiler_params=_cparams(("parallel", "parallel")),
        name="diff_attn",
    )(proj, proj, proj, *tables, lam, subln_g)


def _fcum_kernel(f_ref, bf_ref, cum_ref, cumt_ref, *, blk):
    x = f_ref[...] + bf_ref[...]
    logf = jnp.minimum(x, 0.0) - jnp.log1p(jnp.exp(-jnp.abs(x)))
    r = lax.broadcasted_iota(jnp.int32, (blk, blk), 0)
    c = lax.broadcasted_iota(jnp.int32, (blk, blk), 1)
    tri = (r >= c).astype(F32)
    carry = jnp.zeros((1, LANES), F32)
    for n in range(x.shape[0] // blk):
        part = jnp.dot(tri, logf[n * blk:(n + 1) * blk], preferred_element_type=F32,
                       precision=lax.Precision.HIGHEST) + carry
        cum_ref[n * blk:(n + 1) * blk, :] = part
        carry = part[blk - 1:blk, :]
    cum_t = cum_ref[...].T
    for hh in range(FA_HEADS):
        cumt_ref[hh] = cum_t[hh:hh + 1, :]


def _fcum(f_logit, b_f, B, S):
    T = f_logit.shape[0]
    return pl.pallas_call(
        functools.partial(_fcum_kernel, blk=256),
        grid=(B,),
        in_specs=[pl.BlockSpec((S, LANES), lambda b: (b, 0)),
                  pl.BlockSpec((1, LANES), lambda b: (0, 0))],
        out_specs=[pl.BlockSpec((S, LANES), lambda b: (b, 0)),
                   pl.BlockSpec((FA_HEADS, 1, S), lambda b: (b, 0, 0))],
        out_shape=[jax.ShapeDtypeStruct((T, LANES), F32),
                   jax.ShapeDtypeStruct((B * FA_HEADS, 1, S), F32)],
        compiler_params=_cparams(("parallel",)),
        name="forget_cumsum",
    )(f_logit, b_f)


def _fox_attn_kernel(q_ref, k_ref, v_ref, cum_ref, cumt_ref, o_ref, qs_ref, *, tq):
    S = q_ref.shape[0]
    h = pl.program_id(1)
    qs_ref[...] = (q_ref[...].astype(F32) * (FA_HEAD_DIM ** -0.5)).astype(BF16)
    causal = _causal(tq, tq)
    for i in range(S // tq):
        rows = pl.ds(i * tq, tq)
        cum = cum_ref[rows, :]
        lane = lax.broadcasted_iota(jnp.int32, cum.shape, 1)
        cq = jnp.sum(jnp.where(lane == h, cum, 0.0), axis=-1, keepdims=True)
        q = qs_ref[rows, :]
        s_diag = _qk(q, k_ref[rows, :]) + (cq - cumt_ref[0, :, rows])
        s_diag = jnp.where(causal, s_diag, MASK_VALUE)
        s_off = None
        if i:
            past = pl.ds(0, i * tq)
            s_off = _qk(q, k_ref[past, :]) + (cq - cumt_ref[0, :, past])
        p_off, p_diag, l = _row_softmax(s_off, s_diag)
        o = jnp.dot(p_diag.astype(BF16), v_ref[rows, :], preferred_element_type=F32)
        if i:
            o = o + jnp.dot(p_off.astype(BF16), v_ref[past, :], preferred_element_type=F32)
        o_ref[rows, :] = (o / l).astype(o_ref.dtype)


def _fox_attn(proj, cum, cum_t, B, S, tq=256):
    T = proj.shape[0]
    per_slot = SLOT // HEAD_COLS
    return pl.pallas_call(
        functools.partial(_fox_attn_kernel, tq=tq),
        grid=(B, FA_HEADS),
        in_specs=[pl.BlockSpec((S, HEAD_COLS), lambda b, h: (b, J_CQ * per_slot + h)),
                  pl.BlockSpec((S, HEAD_COLS), lambda b, h: (b, J_CK * per_slot + h)),
                  pl.BlockSpec((S, HEAD_COLS), lambda b, h: (b, J_CV * per_slot + h)),
                  pl.BlockSpec((S, LANES), lambda b, h: (b, 0)),
                  pl.BlockSpec((1, 1, S), lambda b, h: (b * FA_HEADS + h, 0, 0))],
        out_specs=pl.BlockSpec((S, HEAD_COLS), lambda b, h: (b, h)),
        out_shape=jax.ShapeDtypeStruct((T, FA_HEADS * HEAD_COLS), BF16),
        scratch_shapes=[pltpu.VMEM((S, HEAD_COLS), BF16)],
        compiler_params=_cparams(("parallel", "parallel")),
        name="fox_attn",
    )(proj, proj, proj, cum, cum_t)


def _pool_kernel(h_ref, w_ref, sc_ref, o_ref):
    S = h_ref.shape[0]
    t = lax.broadcasted_iota(jnp.int32, (S, HEAD_COLS), 0)
    for g, win in enumerate(POOL_WINDOWS):
        cols = slice(g * HEAD_COLS, (g + 1) * HEAD_COLS)
        x = h_ref[:, cols].astype(F32)
        tot = x
        span = 1
        while span < win:
            tot = tot + jnp.where(t >= span, pltpu.roll(tot, span, 0), 0.0)
            span *= 2
        cnt = jnp.minimum(t + 1, win).astype(F32)
        pooled = (tot / cnt - x).astype(BF16)
        y = jnp.dot(pooled, w_ref[g].astype(BF16), preferred_element_type=F32)
        o_ref[:, cols] = (y * sc_ref[:, cols]).astype(o_ref.dtype)


def _pool(proj, w_pool, scale, B, S):
    T = proj.shape[0]
    return pl.pallas_call(
        _pool_kernel,
        grid=(B,),
        in_specs=[pl.BlockSpec((S, SLOT), lambda b: (b, J_DH)),
                  pl.BlockSpec(w_pool.shape, lambda b: (0, 0, 0)),
                  pl.BlockSpec((1, SLOT), lambda b: (0, 0))],
        out_specs=pl.BlockSpec((S, SLOT), lambda b: (b, 0)),
        out_shape=jax.ShapeDtypeStruct((T, SLOT), BF16),
        compiler_params=_cparams(("parallel",)),
        name="pool",
    )(proj, w_pool, scale)


def _merge_kernel(oa_ref, ob_ref, oc_ref, od_ref, gate_ref, wb_ref, wo_ref, g_ref, h_ref, out_ref):
    D = h_ref.shape[1]
    merged = None
    for n, br in enumerate((oa_ref, ob_ref, oc_ref, od_ref)):
        bd = jnp.dot(br[...], wb_ref[n], preferred_element_type=F32)
        term = jax.nn.sigmoid(gate_ref[:, n * D:(n + 1) * D].astype(F32)) * bd
        merged = term if merged is None else merged + term
    y = jnp.dot(merged.astype(BF16), wo_ref[...], preferred_element_type=F32)
    out_ref[...] = h_ref[...] + _rms(y, g_ref[...])


def _merge(branches, gates, w_branch, w_out, g, h, tm=256):
    T, D = h.shape
    row = lambda i: (i, 0)
    once = pl.Buffered(1)
    return pl.pallas_call(
        _merge_kernel,
        grid=(T // tm,),
        in_specs=[pl.BlockSpec((tm, SLOT), row)] * N_BRANCH + [
            pl.BlockSpec((tm, N_BRANCH * D), row),
            pl.BlockSpec(w_branch.shape, lambda i: (0, 0, 0), pipeline_mode=once),
            pl.BlockSpec(w_out.shape, lambda i: (0, 0), pipeline_mode=once),
            pl.BlockSpec((1, D), lambda i: (0, 0)),
            pl.BlockSpec((tm, D), row)],
        out_specs=pl.BlockSpec((tm, D), row),
        out_shape=jax.ShapeDtypeStruct((T, D), F32),
        compiler_params=_cparams(("parallel",), 48),
        name="merge_out",
    )(*branches, gates, w_branch, w_out, g, h)


def _ffn_kernel(h_ref, g1_ref, wu_ref, wd_ref, g2_ref, out_ref, hn_ref, acc_ref):
    f = pl.program_id(1)

    @pl.when(f == 0)
    def _():
        hn_ref[...] = _rms(h_ref[...], g1_ref[...]).astype(BF16)
        acc_ref[...] = jnp.zeros_like(acc_ref)

    up = jnp.dot(hn_ref[...], wu_ref[...], preferred_element_type=F32)
    a = jnp.square(jnp.maximum(up, 0.0)).astype(BF16)
    acc_ref[...] += jnp.dot(a, wd_ref[...], preferred_element_type=F32)

    @pl.when(f == pl.num_programs(1) - 1)
    def _():
        out_ref[...] = h_ref[...] + _rms(acc_ref[...], g2_ref[...])


def _ffn(h, g1, w_up, w_down, g2, tm=512, tf=512):
    T, D = h.shape
    F = w_up.shape[1]
    row = lambda i, f: (i, 0)
    vec = lambda i, f: (0, 0)
    return pl.pallas_call(
        _ffn_kernel,
        grid=(T // tm, F // tf),
        in_specs=[pl.BlockSpec((tm, D), row),
                  pl.BlockSpec((1, D), vec),
                  pl.BlockSpec((D, tf), lambda i, f: (0, f)),
                  pl.BlockSpec((tf, D), lambda i, f: (f, 0)),
                  pl.BlockSpec((1, D), vec)],
        out_specs=pl.BlockSpec((tm, D), row),
        out_shape=jax.ShapeDtypeStruct((T, D), F32),
        scratch_shapes=[pltpu.VMEM((tm, D), BF16), pltpu.VMEM((tm, D), F32)],
        compiler_params=_cparams(("parallel", "arbitrary"), 48),
        name="ffn",
    )(h, g1, w_up, w_down, g2)


def kernel(x, positions, norm_mix_pre, norm_mix_post, norm_ffn_pre, norm_ffn_post, w_in, gm_ln_g,
           gm_ln_b, gm_w_s, gm_b_s, da_lambda, da_subln_g, fa_b_f, pool_w, pool_scale, w_branch,
           w_out, w_ffn_up, w_ffn_down):
    B, S, D = x.shape
    T = B * S
    h = x.reshape(T, D)
    tables = _rope_tables(positions)
    row = lambda a: a.reshape(1, -1)
    for l in range(DEPTH):
        lambda_init = 0.8 - 0.6 * math.exp(-0.3 * l)
        f0 = N_ALIGNED * SLOT
        w_f = jnp.pad(w_in[l, :, f0:f0 + FA_HEADS], ((0, 0), (0, LANES - FA_HEADS))).astype(BF16)
        proj, f_logit = _inproj(h, row(norm_mix_pre[l]), _prep_w_in(w_in, l), w_f)

        o_a = _gmlp(proj, row(gm_ln_g[l]), row(gm_ln_b[l]), gm_w_s[l], gm_b_s[l].T)
        o_b = _diff_attn(proj, tables, da_lambda[l], row(da_subln_g[l]), B, S, lambda_init)
        b_f = jnp.pad(fa_b_f[l], (0, LANES - FA_HEADS)).reshape(1, LANES)
        cum, cum_t = _fcum(f_logit, b_f, B, S)
        o_c = _fox_attn(proj, cum, cum_t, B, S)
        o_d = _pool(proj, pool_w[l], row(pool_scale[l]), B, S)

        h = _merge((o_a, o_b, o_c, o_d), proj, w_branch[l].astype(BF16), w_out[l].astype(BF16),
                   row(norm_mix_post[l]), h)
        h = _ffn(h, row(norm_ffn_pre[l]), w_ffn_up[l].astype(BF16), w_ffn_down[l].astype(BF16),
                 row(norm_ffn_post[l]))
    return h.reshape(B, S, D)
```

```python
import functools
import math

import jax
import jax.numpy as jnp
from jax import lax
from jax.experimental import pallas as pl
from jax.experimental.pallas import tpu as pltpu

F32 = jnp.float32
BF16 = jnp.bfloat16

DEPTH = 2
N_BRANCH = 4
GM_GROUPS = 4
GM_CHUNK = 128
DA_HEADS = 4
DA_QK_DIM = 64
DA_ROT_DIM = 16
FA_HEADS = 4
FA_HEAD_DIM = 128
POOL_WINDOWS = (2, 4, 8, 16)
ROPE_THETA = 500000.0
NORM_EPS = 1e-6
MASK_VALUE = -1e30

LANES = 128
SUBLANES = 8
HEAD_COLS = 128
SLOT = 512

N_GATE_SLOTS = 16
J_U, J_V, J_BQ, J_BK, J_BV, J_CQ, J_CK, J_CV, J_DH = range(N_GATE_SLOTS, N_GATE_SLOTS + 9)
N_SLOTS = J_DH + 1
N_ALIGNED = 8


def _rms(x, g):
    return x * lax.rsqrt(jnp.mean(x * x, axis=-1, keepdims=True) + NORM_EPS) * g


def _cparams(sem, vmem_mb=None):
    kw = dict(dimension_semantics=sem)
    if vmem_mb is not None:
        kw["vmem_limit_bytes"] = vmem_mb * 1024 * 1024
    return pltpu.CompilerParams(**kw)


def _rope_kernel(pos_ref, inv_ref, c_ref, sa_ref, sb_ref):
    ang = pos_ref[...].astype(F32) * inv_ref[...]
    sub = lax.broadcasted_iota(jnp.int32, ang.shape, 1) % DA_QK_DIM
    half = DA_ROT_DIM // 2
    s = jnp.sin(ang)
    c_ref[...] = jnp.cos(ang)
    sa_ref[...] = jnp.where(sub < half, -s, 0.0)
    sb_ref[...] = jnp.where((sub >= half) & (sub < 2 * half), s, 0.0)


def _rope_tables(positions):
    T = positions.size
    tm = 1024
    half = DA_ROT_DIM // 2
    inv = 1.0 / (ROPE_THETA ** (jnp.arange(0, DA_ROT_DIM, 2, dtype=F32) / DA_ROT_DIM))
    sub = jnp.arange(LANES) % DA_QK_DIM
    inv_lane = jnp.where(sub < 2 * half, inv[sub % half], 0.0).reshape(1, LANES)
    tab = jax.ShapeDtypeStruct((T, LANES), F32)
    return pl.pallas_call(
        _rope_kernel,
        grid=(T // tm,),
        in_specs=[pl.BlockSpec((tm, 1), lambda i: (i, 0)),
                  pl.BlockSpec((1, LANES), lambda i: (0, 0))],
        out_specs=[pl.BlockSpec((tm, LANES), lambda i: (i, 0))] * 3,
        out_shape=[tab] * 3,
        compiler_params=_cparams(("parallel",)),
        name="rope_tables",
    )(positions.reshape(T, 1), inv_lane)


def _rotate(x, c, sa, sb):
    half = DA_ROT_DIM // 2
    return x * c + pltpu.roll(x, LANES - half, 1) * sa + pltpu.roll(x, half, 1) * sb


def _wprep_kernel(a_ref, o_ref):
    n_layers, tn, _ = o_ref.shape
    sub = lax.broadcasted_iota(jnp.int32, (SUBLANES, LANES), 0)
    low = {d: (sub & d) == 0 for d in (4, 2, 1)}

    def transpose8(a):
        for d in (4, 2, 1):
            nxt = list(a)
            for i in range(SUBLANES):
                if not i & d:
                    x, y = a[i], a[i + d]
                    nxt[i] = jnp.where(low[d], x, pltpu.roll(y, d, 0))
                    nxt[i + d] = jnp.where(low[d], pltpu.roll(x, SUBLANES - d, 0), y)
            a = nxt
        return a

    rows_per_iter = 2 * SUBLANES

    def body(it, carry):
        n0 = pl.multiple_of(it * rows_per_iter, rows_per_iter)
        for jb in range(a_ref.shape[1] // SUBLANES):
            js = pl.ds(jb * SUBLANES, SUBLANES)
            halves = [transpose8([a_ref[n0 + half * SUBLANES + i, js, :] for i in range(SUBLANES)])
                      for half in range(2)]
            for s in range(SUBLANES):
                kb, l = divmod(jb * SUBLANES + s, n_layers)
                tile = jnp.concatenate([halves[0][s], halves[1][s]], axis=0)
                o_ref[l, pl.ds(n0, rows_per_iter), kb * LANES:(kb + 1) * LANES] = tile.astype(o_ref.dtype)
        return carry

    lax.fori_loop(0, tn // rows_per_iter, body, 0)


def _native_rows(w_in):
    L, D, N = w_in.shape
    v = jnp.transpose(w_in.reshape(L, D // LANES, LANES, N), (3, 1, 0, 2))
    return v.reshape(N, (D // LANES) * L, LANES)


def _prep_rows(w_rows, L, D, tn, n, offset):
    return pl.pallas_call(
        _wprep_kernel,
        grid=(n,),
        in_specs=[pl.BlockSpec((pl.Element(tn), pl.Element(w_rows.shape[1]), pl.Element(LANES)),
                               lambda j: (offset(j), 0, 0))],
        out_specs=pl.BlockSpec((L, tn, D), lambda j: (0, j, 0)),
        out_shape=jax.ShapeDtypeStruct((L, n * tn, D), BF16),
        compiler_params=_cparams(("parallel",), 40),
        name="w_in_prep",
    )(w_rows)


def _prep_w_in(w_in):
    L, D, _ = w_in.shape
    f0 = N_ALIGNED * SLOT
    d0 = f0 + FA_HEADS
    g0 = d0 + SLOT

    def slot_row(j):
        return jnp.where(j < N_GATE_SLOTS, g0 + SLOT * j, jnp.where(j == J_DH, d0, SLOT * (j - N_GATE_SLOTS)))

    w_rows = _native_rows(w_in)
    return (_prep_rows(w_rows, L, D, SLOT, N_SLOTS, slot_row),
            _prep_rows(w_rows, L, D, LANES, 1, lambda j: f0 + 0 * j))


def _dot_nt(a, b):
    return lax.dot_general(a, b, (((1,), (1,)), ((), ())), preferred_element_type=F32)


def _inproj_kernel(h_ref, g_ref, w_ref, wf_ref, out_ref, f_ref, xn_ref):
    @pl.when(pl.program_id(1) == 0)
    def _():
        xn_ref[...] = _rms(h_ref[...], g_ref[...]).astype(BF16)
        f_ref[...] = _dot_nt(xn_ref[...], wf_ref[...])

    out_ref[...] = _dot_nt(xn_ref[...], w_ref[...]).astype(out_ref.dtype)


def _inproj(h, g, w_t, wf_t, l, tm=1024):
    T, D = h.shape
    row = lambda i, j: (i, 0)
    return pl.pallas_call(
        _inproj_kernel,
        grid=(T // tm, N_SLOTS),
        in_specs=[pl.BlockSpec((tm, D), row),
                  pl.BlockSpec((1, D), lambda i, j: (0, 0)),
                  pl.BlockSpec((None, SLOT, D), lambda i, j: (l, j, 0)),
                  pl.BlockSpec((None, LANES, D), lambda i, j: (l, 0, 0))],
        out_specs=[pl.BlockSpec((tm, SLOT), lambda i, j: (i, j)),
                   pl.BlockSpec((tm, LANES), row)],
        out_shape=[jax.ShapeDtypeStruct((T, N_SLOTS * SLOT), BF16),
                   jax.ShapeDtypeStruct((T, LANES), F32)],
        scratch_shapes=[pltpu.VMEM((tm, D), BF16)],
        compiler_params=_cparams(("parallel", "arbitrary"), 48),
        name="inproj",
    )(h, g, w_t, wf_t)


def _gmlp_kernel(u_ref, v_ref, lng_ref, lnb_ref, ws_ref, bst_ref, o_ref):
    v = v_ref[...].astype(F32)
    mu = jnp.mean(v, axis=-1, keepdims=True)
    vc = v - mu
    var = jnp.mean(vc * vc, axis=-1, keepdims=True)
    vn = (vc * lax.rsqrt(var + NORM_EPS) * lng_ref[...] + lnb_ref[...]).astype(BF16)
    r = lax.broadcasted_iota(jnp.int32, (GM_CHUNK, GM_CHUNK), 0)
    c = lax.broadcasted_iota(jnp.int32, (GM_CHUNK, GM_CHUNK), 1)
    causal = r >= c
    for g in range(GM_GROUPS):
        wg = jnp.where(causal, ws_ref[g], 0.0).astype(BF16)
        bcol = bst_ref[:, g:g + 1]
        cols = slice(g * HEAD_COLS, (g + 1) * HEAD_COLS)
        for n in range(v.shape[0] // GM_CHUNK):
            rows = slice(n * GM_CHUNK, (n + 1) * GM_CHUNK)
            mixed = jnp.dot(wg, vn[rows, cols], preferred_element_type=F32) + bcol
            o_ref[rows, cols] = (u_ref[rows, cols].astype(F32) * mixed).astype(o_ref.dtype)


def _gmlp(proj, ln_g, ln_b, w_s, b_s_t, tm=512):
    T = proj.shape[0]
    const2 = lambda i: (0, 0)
    return pl.pallas_call(
        _gmlp_kernel,
        grid=(T // tm,),
        in_specs=[pl.BlockSpec((tm, SLOT), lambda i: (i, J_U)),
                  pl.BlockSpec((tm, SLOT), lambda i: (i, J_V)),
                  pl.BlockSpec((1, SLOT), const2),
                  pl.BlockSpec((1, SLOT), const2),
                  pl.BlockSpec((GM_GROUPS, GM_CHUNK, GM_CHUNK), lambda i: (0, 0, 0)),
                  pl.BlockSpec((GM_CHUNK, GM_GROUPS), const2)],
        out_specs=pl.BlockSpec((tm, SLOT), lambda i: (i, 0)),
        out_shape=jax.ShapeDtypeStruct((T, SLOT), BF16),
        compiler_params=_cparams(("parallel",)),
        name="gmlp",
    )(proj, proj, ln_g, ln_b, w_s, b_s_t)


def _qk(q, k):
    return lax.dot_general(q, k, (((1,), (1,)), ((), ())), preferred_element_type=F32)


def _causal(rows, tq):
    r = lax.broadcasted_iota(jnp.int32, (rows, tq), 0) % tq
    c = lax.broadcasted_iota(jnp.int32, (rows, tq), 1)
    return r >= c


def _row_softmax(s_off, s_diag):
    m = jnp.max(s_diag, axis=-1, keepdims=True)
    if s_off is not None:
        m = jnp.maximum(m, jnp.max(s_off, axis=-1, keepdims=True))
    p_diag = jnp.exp(s_diag - m)
    l = jnp.sum(p_diag, axis=-1, keepdims=True)
    p_off = None
    if s_off is not None:
        p_off = jnp.exp(s_off - m)
        l = l + jnp.sum(p_off, axis=-1, keepdims=True)
    return p_off, p_diag, l


def _diff_attn_kernel(q_ref, k_ref, v_ref, c_ref, sa_ref, sb_ref, lam_ref, g_ref, o_ref,
                      qs_ref, kr_ref, *, tq, lambda_init):
    S = q_ref.shape[0]
    c, sa, sb = c_ref[...], sa_ref[...], sb_ref[...]
    qf = _rotate(q_ref[...].astype(F32), c, sa, sb) * (DA_QK_DIM ** -0.5)
    lane = lax.broadcasted_iota(jnp.int32, qf.shape, 1)
    qs_ref[0] = jnp.where(lane < DA_QK_DIM, qf, 0.0).astype(BF16)
    qs_ref[1] = jnp.where(lane >= DA_QK_DIM, qf, 0.0).astype(BF16)
    kr_ref[...] = _rotate(k_ref[...].astype(F32), c, sa, sb).astype(BF16)

    lp = lam_ref[...]

    def total(x):
        return jnp.sum(jnp.sum(x, axis=1, keepdims=True), axis=0, keepdims=True)

    lam = jnp.exp(total(lp[0:1] * lp[1:2])) - jnp.exp(total(lp[2:3] * lp[3:4])) + lambda_init
    causal = _causal(2 * tq, tq)
    g = g_ref[...]

    def combine(p, r):
        return (p[:tq] * r[:tq] - p[tq:] * (lam * r[tq:])).astype(BF16)

    for i in range(S // tq):
        rows = pl.ds(i * tq, tq)
        q2 = jnp.concatenate([qs_ref[0, rows, :], qs_ref[1, rows, :]], axis=0)
        s_diag = jnp.where(causal, _qk(q2, kr_ref[rows, :]), MASK_VALUE)
        s_off = _qk(q2, kr_ref[pl.ds(0, i * tq), :]) if i else None
        p_off, p_diag, l = _row_softmax(s_off, s_diag)
        r = 1.0 / l
        o = jnp.dot(combine(p_diag, r), v_ref[rows, :], preferred_element_type=F32)
        if i:
            o = o + jnp.dot(combine(p_off, r), v_ref[pl.ds(0, i * tq), :], preferred_element_type=F32)
        o_ref[rows, :] = (_rms(o, g) * (1.0 - lambda_init)).astype(o_ref.dtype)


def _diff_attn(proj, tables, lam, subln_g, B, S, lambda_init, tq=256):
    T = proj.shape[0]
    per_slot = SLOT // HEAD_COLS
    seq = lambda b, h: (b, 0)
    kern = functools.partial(_diff_attn_kernel, tq=tq, lambda_init=lambda_init)
    return pl.pallas_call(
        kern,
        grid=(B, DA_HEADS),
        in_specs=[pl.BlockSpec((S, HEAD_COLS), lambda b, h: (b, J_BQ * per_slot + h)),
                  pl.BlockSpec((S, HEAD_COLS), lambda b, h: (b, J_BK * per_slot + h)),
                  pl.BlockSpec((S, HEAD_COLS), lambda b, h: (b, J_BV * per_slot + h)),
                  pl.BlockSpec((S, LANES), seq),
                  pl.BlockSpec((S, LANES), seq),
                  pl.BlockSpec((S, LANES), seq),
                  pl.BlockSpec(lam.shape, lambda b, h: (0, 0)),
                  pl.BlockSpec((1, HEAD_COLS), lambda b, h: (0, 0))],
        out_specs=pl.BlockSpec((S, HEAD_COLS), lambda b, h: (b, h)),
        out_shape=jax.ShapeDtypeStruct((T, DA_HEADS * HEAD_COLS), BF16),
        scratch_shapes=[pltpu.VMEM((2, S, HEAD_COLS), BF16), pltpu.VMEM((S, HEAD_COLS), BF16)],
        compiler_params=_cparams(("parallel", "parallel")),
        name="diff_attn",
    )(proj, proj, proj, *tables, lam, subln_g)


def _fcum_kernel(f_ref, bf_ref, cum_ref, cumt_ref, *, blk):
    x = f_ref[...] + bf_ref[...]
    logf = jnp.minimum(x, 0.0) - jnp.log1p(jnp.exp(-jnp.abs(x)))
    r = lax.broadcasted_iota(jnp.int32, (blk, blk), 0)
    c = lax.broadcasted_iota(jnp.int32, (blk, blk), 1)
    tri = (r >= c).astype(F32)
    carry = jnp.zeros((1, LANES), F32)
    for n in range(x.shape[0] // blk):
        part = jnp.dot(tri, logf[n * blk:(n + 1) * blk], preferred_element_type=F32,
                       precision=lax.Precision.HIGHEST) + carry
        cum_ref[n * blk:(n + 1) * blk, :] = part
        carry = part[blk - 1:blk, :]
    cum_t = cum_ref[...].T
    for hh in range(FA_HEADS):
        cumt_ref[hh] = cum_t[hh:hh + 1, :]


def _fcum(f_logit, b_f, B, S):
    T = f_logit.shape[0]
    return pl.pallas_call(
        functools.partial(_fcum_kernel, blk=256),
        grid=(B,),
        in_specs=[pl.BlockSpec((S, LANES), lambda b: (b, 0)),
                  pl.BlockSpec((1, LANES), lambda b: (0, 0))],
        out_specs=[pl.BlockSpec((S, LANES), lambda b: (b, 0)),
                   pl.BlockSpec((FA_HEADS, 1, S), lambda b: (b, 0, 0))],
        out_shape=[jax.ShapeDtypeStruct((T, LANES), F32),
                   jax.ShapeDtypeStruct((B * FA_HEADS, 1, S), F32)],
        compiler_params=_cparams(("parallel",)),
        name="forget_cumsum",
    )(f_logit, b_f)


def _fox_attn_kernel(q_ref, k_ref, v_ref, cum_ref, cumt_ref, o_ref, qs_ref, *, tq):
    S = q_ref.shape[0]
    h = pl.program_id(1)
    qs_ref[...] = (q_ref[...].astype(F32) * (FA_HEAD_DIM ** -0.5)).astype(BF16)
    causal = _causal(tq, tq)
    for i in range(S // tq):
        rows = pl.ds(i * tq, tq)
        cum = cum_ref[rows, :]
        lane = lax.broadcasted_iota(jnp.int32, cum.shape, 1)
        cq = jnp.sum(jnp.where(lane == h, cum, 0.0), axis=-1, keepdims=True)
        q = qs_ref[rows, :]
        s_diag = _qk(q, k_ref[rows, :]) + (cq - cumt_ref[0, :, rows])
        s_diag = jnp.where(causal, s_diag, MASK_VALUE)
        s_off = None
        if i:
            past = pl.ds(0, i * tq)
            s_off = _qk(q, k_ref[past, :]) + (cq - cumt_ref[0, :, past])
        p_off, p_diag, l = _row_softmax(s_off, s_diag)
        o = jnp.dot(p_diag.astype(BF16), v_ref[rows, :], preferred_element_type=F32)
        if i:
            o = o + jnp.dot(p_off.astype(BF16), v_ref[past, :], preferred_element_type=F32)
        o_ref[rows, :] = (o / l).astype(o_ref.dtype)


def _fox_attn(proj, cum, cum_t, B, S, tq=256):
    T = proj.shape[0]
    per_slot = SLOT // HEAD_COLS
    return pl.pallas_call(
        functools.partial(_fox_attn_kernel, tq=tq),
        grid=(B, FA_HEADS),
        in_specs=[pl.BlockSpec((S, HEAD_COLS), lambda b, h: (b, J_CQ * per_slot + h)),
                  pl.BlockSpec((S, HEAD_COLS), lambda b, h: (b, J_CK * per_slot + h)),
                  pl.BlockSpec((S, HEAD_COLS), lambda b, h: (b, J_CV * per_slot + h)),
                  pl.BlockSpec((S, LANES), lambda b, h: (b, 0)),
                  pl.BlockSpec((1, 1, S), lambda b, h: (b * FA_HEADS + h, 0, 0))],
        out_specs=pl.BlockSpec((S, HEAD_COLS), lambda b, h: (b, h)),
        out_shape=jax.ShapeDtypeStruct((T, FA_HEADS * HEAD_COLS), BF16),
        scratch_shapes=[pltpu.VMEM((S, HEAD_COLS), BF16)],
        compiler_params=_cparams(("parallel", "parallel")),
        name="fox_attn",
    )(proj, proj, proj, cum, cum_t)


def _pool_kernel(h_ref, w_ref, sc_ref, o_ref):
    S = h_ref.shape[0]
    t = lax.broadcasted_iota(jnp.int32, (S, HEAD_COLS), 0)
    for g, win in enumerate(POOL_WINDOWS):
        cols = slice(g * HEAD_COLS, (g + 1) * HEAD_COLS)
        x = h_ref[:, cols].astype(F32)
        tot = x
        span = 1
        while span < win:
            tot = tot + jnp.where(t >= span, pltpu.roll(tot, span, 0), 0.0)
            span *= 2
        cnt = jnp.minimum(t + 1, win).astype(F32)
        pooled = (tot / cnt - x).astype(BF16)
        y = jnp.dot(pooled, w_ref[g].astype(BF16), preferred_element_type=F32)
        o_ref[:, cols] = (y * sc_ref[:, cols]).astype(o_ref.dtype)


def _pool(proj, w_pool, scale, B, S):
    T = proj.shape[0]
    return pl.pallas_call(
        _pool_kernel,
        grid=(B,),
        in_specs=[pl.BlockSpec((S, SLOT), lambda b: (b, J_DH)),
                  pl.BlockSpec(w_pool.shape, lambda b: (0, 0, 0)),
                  pl.BlockSpec((1, SLOT), lambda b: (0, 0))],
        out_specs=pl.BlockSpec((S, SLOT), lambda b: (b, 0)),
        out_shape=jax.ShapeDtypeStruct((T, SLOT), BF16),
        compiler_params=_cparams(("parallel",)),
        name="pool",
    )(proj, w_pool, scale)


def _merge_kernel(oa_ref, ob_ref, oc_ref, od_ref, gate_ref, wb_ref, wo_ref, g_ref, h_ref, out_ref):
    D = h_ref.shape[1]
    merged = None
    for n, br in enumerate((oa_ref, ob_ref, oc_ref, od_ref)):
        bd = jnp.dot(br[...], wb_ref[n], preferred_element_type=F32)
        term = jax.nn.sigmoid(gate_ref[:, n * D:(n + 1) * D].astype(F32)) * bd
        merged = term if merged is None else merged + term
    y = jnp.dot(merged.astype(BF16), wo_ref[...], preferred_element_type=F32)
    out_ref[...] = h_ref[...] + _rms(y, g_ref[...])


def _merge(branches, gates, w_branch, w_out, l, g, h, tm=256):
    T, D = h.shape
    row = lambda i: (i, 0)
    once = pl.Buffered(1)
    return pl.pallas_call(
        _merge_kernel,
        grid=(T // tm,),
        in_specs=[pl.BlockSpec((tm, SLOT), row)] * N_BRANCH + [
            pl.BlockSpec((tm, N_BRANCH * D), row),
            pl.BlockSpec((None,) + w_branch.shape[1:], lambda i: (l, 0, 0, 0), pipeline_mode=once),
            pl.BlockSpec((None,) + w_out.shape[1:], lambda i: (l, 0, 0), pipeline_mode=once),
            pl.BlockSpec((1, D), lambda i: (0, 0)),
            pl.BlockSpec((tm, D), row)],
        out_specs=pl.BlockSpec((tm, D), row),
        out_shape=jax.ShapeDtypeStruct((T, D), F32),
        compiler_params=_cparams(("parallel",), 48),
        name="merge_out",
    )(*branches, gates, w_branch, w_out, g, h)


def _ffn_kernel(h_ref, g1_ref, wu_ref, wd_ref, g2_ref, out_ref, hn_ref, acc_ref):
    f = pl.program_id(1)

    @pl.when(f == 0)
    def _():
        hn_ref[...] = _rms(h_ref[...], g1_ref[...]).astype(BF16)
        acc_ref[...] = jnp.zeros_like(acc_ref)

    up = jnp.dot(hn_ref[...], wu_ref[...], preferred_element_type=F32)
    a = jnp.square(jnp.maximum(up, 0.0)).astype(BF16)
    acc_ref[...] += jnp.dot(a, wd_ref[...], preferred_element_type=F32)

    @pl.when(f == pl.num_programs(1) - 1)
    def _():
        out_ref[...] = h_ref[...] + _rms(acc_ref[...], g2_ref[...])


def _ffn(h, g1, w_up, w_down, l, g2, tm=512, tf=1024):
    T, D = h.shape
    F = w_up.shape[2]
    row = lambda i, f: (i, 0)
    vec = lambda i, f: (0, 0)
    return pl.pallas_call(
        _ffn_kernel,
        grid=(T // tm, F // tf),
        in_specs=[pl.BlockSpec((tm, D), row),
                  pl.BlockSpec((1, D), vec),
                  pl.BlockSpec((None, D, tf), lambda i, f: (l, 0, f)),
                  pl.BlockSpec((None, tf, D), lambda i, f: (l, f, 0)),
                  pl.BlockSpec((1, D), vec)],
        out_specs=pl.BlockSpec((tm, D), row),
        out_shape=jax.ShapeDtypeStruct((T, D), F32),
        scratch_shapes=[pltpu.VMEM((tm, D), BF16), pltpu.VMEM((tm, D), F32)],
        compiler_params=_cparams(("parallel", "arbitrary"), 48),
        name="ffn",
    )(h, g1, w_up, w_down, g2)


def kernel(x, positions, norm_mix_pre, norm_mix_post, norm_ffn_pre, norm_ffn_post, w_in, gm_ln_g,
           gm_ln_b, gm_w_s, gm_b_s, da_lambda, da_subln_g, fa_b_f, pool_w, pool_scale, w_branch,
           w_out, w_ffn_up, w_ffn_down):
    B, S, D = x.shape
    T = B * S
    h = x.reshape(T, D)
    tables = _rope_tables(positions)
    w_t, wf_t = _prep_w_in(w_in)
    w_branch, w_out, w_ffn_up, w_ffn_down = (w.astype(BF16) for w in (w_branch, w_out, w_ffn_up, w_ffn_down))
    row = lambda a: a.reshape(1, -1)
    for l in range(DEPTH):
        lambda_init = 0.8 - 0.6 * math.exp(-0.3 * l)
        proj, f_logit = _inproj(h, row(norm_mix_pre[l]), w_t, wf_t, l)

        o_a = _gmlp(proj, row(gm_ln_g[l]), row(gm_ln_b[l]), gm_w_s[l], gm_b_s[l].T)
        o_b = _diff_attn(proj, tables, da_lambda[l], row(da_subln_g[l]), B, S, lambda_init)
        b_f = jnp.pad(fa_b_f[l], (0, LANES - FA_HEADS)).reshape(1, LANES)
        cum, cum_t = _fcum(f_logit, b_f, B, S)
        o_c = _fox_attn(proj, cum, cum_t, B, S)
        o_d = _pool(proj, pool_w[l], row(pool_scale[l]), B, S)

        h = _merge((o_a, o_b, o_c, o_d), proj, w_branch, w_out, l, row(norm_mix_post[l]), h)
        h = _ffn(h, row(norm_ffn_pre[l]), w_ffn_up, w_ffn_down, l, row(norm_ffn_post[l]))
    return h.reshape(B, S, D)
```

```python
import functools
import math

import jax
import jax.numpy as jnp
from jax import lax
from jax.experimental import pallas as pl
from jax.experimental.pallas import tpu as pltpu

F32 = jnp.float32
BF16 = jnp.bfloat16

DEPTH = 2
N_BRANCH = 4
GM_GROUPS = 4
GM_CHUNK = 128
DA_HEADS = 4
DA_QK_DIM = 64
DA_ROT_DIM = 16
FA_HEADS = 4
FA_HEAD_DIM = 128
POOL_WINDOWS = (2, 4, 8, 16)
ROPE_THETA = 500000.0
NORM_EPS = 1e-6
MASK_VALUE = -1e30
LOG2E = math.log2(math.e)

LANES = 128
SUBLANES = 8
HEAD_COLS = 128
SLOT = 512

N_GATE_SLOTS = 16
J_U, J_V, J_BQ, J_BK, J_BV, J_CQ, J_CK, J_CV, J_DH = range(N_GATE_SLOTS, N_GATE_SLOTS + 9)
N_SLOTS = J_DH + 1
N_ALIGNED = 8


def _rms(x, g):
    return x * lax.rsqrt(jnp.mean(x * x, axis=-1, keepdims=True) + NORM_EPS) * g


def _cparams(sem, vmem_mb=None):
    kw = dict(dimension_semantics=sem)
    if vmem_mb is not None:
        kw["vmem_limit_bytes"] = vmem_mb * 1024 * 1024
    return pltpu.CompilerParams(**kw)


def _rope_kernel(pos_ref, inv_ref, c_ref, sa_ref, sb_ref):
    ang = pos_ref[...].astype(F32) * inv_ref[...]
    sub = lax.broadcasted_iota(jnp.int32, ang.shape, 1) % DA_QK_DIM
    half = DA_ROT_DIM // 2
    s = jnp.sin(ang)
    c_ref[...] = jnp.cos(ang)
    sa_ref[...] = jnp.where(sub < half, -s, 0.0)
    sb_ref[...] = jnp.where((sub >= half) & (sub < 2 * half), s, 0.0)


def _rope_tables(positions):
    T = positions.size
    tm = 1024
    half = DA_ROT_DIM // 2
    inv = 1.0 / (ROPE_THETA ** (jnp.arange(0, DA_ROT_DIM, 2, dtype=F32) / DA_ROT_DIM))
    sub = jnp.arange(LANES) % DA_QK_DIM
    inv_lane = jnp.where(sub < 2 * half, inv[sub % half], 0.0).reshape(1, LANES)
    tab = jax.ShapeDtypeStruct((T, LANES), F32)
    return pl.pallas_call(
        _rope_kernel,
        grid=(T // tm,),
        in_specs=[pl.BlockSpec((tm, 1), lambda i: (i, 0)),
                  pl.BlockSpec((1, LANES), lambda i: (0, 0))],
        out_specs=[pl.BlockSpec((tm, LANES), lambda i: (i, 0))] * 3,
        out_shape=[tab] * 3,
        compiler_params=_cparams(("parallel",)),
        name="rope_tables",
    )(positions.reshape(T, 1), inv_lane)


def _rotate(x, c, sa, sb):
    half = DA_ROT_DIM // 2
    return x * c + pltpu.roll(x, LANES - half, 1) * sa + pltpu.roll(x, half, 1) * sb


def _wprep_kernel(a_ref, o_ref):
    n_layers, tn, _ = o_ref.shape
    sub = lax.broadcasted_iota(jnp.int32, (SUBLANES, LANES), 0)
    low = {d: (sub & d) == 0 for d in (4, 2, 1)}

    def transpose8(a):
        for d in (4, 2, 1):
            nxt = list(a)
            for i in range(SUBLANES):
                if not i & d:
                    x, y = a[i], a[i + d]
                    nxt[i] = jnp.where(low[d], x, pltpu.roll(y, d, 0))
                    nxt[i + d] = jnp.where(low[d], pltpu.roll(x, SUBLANES - d, 0), y)
            a = nxt
        return a

    rows_per_iter = 2 * SUBLANES

    def body(it, carry):
        n0 = pl.multiple_of(it * rows_per_iter, rows_per_iter)
        for jb in range(a_ref.shape[1] // SUBLANES):
            js = pl.ds(jb * SUBLANES, SUBLANES)
            halves = [transpose8([a_ref[n0 + half * SUBLANES + i, js, :] for i in range(SUBLANES)])
                      for half in range(2)]
            for s in range(SUBLANES):
                kb, l = divmod(jb * SUBLANES + s, n_layers)
                tile = jnp.concatenate([halves[0][s], halves[1][s]], axis=0)
                o_ref[l, pl.ds(n0, rows_per_iter), kb * LANES:(kb + 1) * LANES] = tile.astype(o_ref.dtype)
        return carry

    lax.fori_loop(0, tn // rows_per_iter, body, 0)


def _native_rows(w_in):
    L, D, N = w_in.shape
    v = jnp.transpose(w_in.reshape(L, D // LANES, LANES, N), (3, 1, 0, 2))
    return v.reshape(N, (D // LANES) * L, LANES)


def _prep_rows(w_rows, L, D, tn, n, offset):
    return pl.pallas_call(
        _wprep_kernel,
        grid=(n,),
        in_specs=[pl.BlockSpec((pl.Element(tn), pl.Element(w_rows.shape[1]), pl.Element(LANES)),
                               lambda j: (offset(j), 0, 0))],
        out_specs=pl.BlockSpec((L, tn, D), lambda j: (0, j, 0)),
        out_shape=jax.ShapeDtypeStruct((L, n * tn, D), BF16),
        compiler_params=_cparams(("parallel",), 40),
        name="w_in_prep",
    )(w_rows)


def _prep_w_in(w_in):
    L, D, _ = w_in.shape
    f0 = N_ALIGNED * SLOT
    d0 = f0 + FA_HEADS
    g0 = d0 + SLOT

    def slot_row(j):
        return jnp.where(j < N_GATE_SLOTS, g0 + SLOT * j, jnp.where(j == J_DH, d0, SLOT * (j - N_GATE_SLOTS)))

    w_rows = _native_rows(w_in)
    return (_prep_rows(w_rows, L, D, SLOT, N_SLOTS, slot_row),
            _prep_rows(w_rows, L, D, LANES, 1, lambda j: f0 + 0 * j))


def _dot_nt(a, b):
    return lax.dot_general(a, b, (((1,), (1,)), ((), ())), preferred_element_type=F32)


def _inproj_kernel(h_ref, g_ref, w_ref, wf_ref, out_ref, f_ref, xn_ref):
    @pl.when(pl.program_id(1) == 0)
    def _():
        xn_ref[...] = _rms(h_ref[...], g_ref[...]).astype(BF16)
        f_ref[...] = _dot_nt(xn_ref[...], wf_ref[...])

    out_ref[...] = _dot_nt(xn_ref[...], w_ref[...]).astype(out_ref.dtype)


def _inproj(h, g, w_t, wf_t, l, tm=1024, tn=1280):
    T, D = h.shape
    N = w_t.shape[1]
    row = lambda i, j: (i, 0)
    return pl.pallas_call(
        _inproj_kernel,
        grid=(T // tm, N // tn),
        in_specs=[pl.BlockSpec((tm, D), row),
                  pl.BlockSpec((1, D), lambda i, j: (0, 0)),
                  pl.BlockSpec((None, tn, D), lambda i, j: (l, j, 0)),
                  pl.BlockSpec((None, LANES, D), lambda i, j: (l, 0, 0))],
        out_specs=[pl.BlockSpec((tm, tn), lambda i, j: (i, j)),
                   pl.BlockSpec((tm, LANES), row)],
        out_shape=[jax.ShapeDtypeStruct((T, N), BF16),
                   jax.ShapeDtypeStruct((T, LANES), F32)],
        scratch_shapes=[pltpu.VMEM((tm, D), BF16)],
        compiler_params=_cparams(("parallel", "arbitrary"), 56),
        name="inproj",
    )(h, g, w_t, wf_t)


def _gmlp_kernel(u_ref, v_ref, lng_ref, lnb_ref, ws_ref, bst_ref, o_ref):
    v = v_ref[...].astype(F32)
    mu = jnp.mean(v, axis=-1, keepdims=True)
    vc = v - mu
    var = jnp.mean(vc * vc, axis=-1, keepdims=True)
    vn = (vc * lax.rsqrt(var + NORM_EPS) * lng_ref[...] + lnb_ref[...]).astype(BF16)
    r = lax.broadcasted_iota(jnp.int32, (GM_CHUNK, GM_CHUNK), 0)
    c = lax.broadcasted_iota(jnp.int32, (GM_CHUNK, GM_CHUNK), 1)
    causal = r >= c
    for g in range(GM_GROUPS):
        wg = jnp.where(causal, ws_ref[g], 0.0).astype(BF16)
        bcol = bst_ref[:, g:g + 1]
        cols = slice(g * HEAD_COLS, (g + 1) * HEAD_COLS)
        for n in range(v.shape[0] // GM_CHUNK):
            rows = slice(n * GM_CHUNK, (n + 1) * GM_CHUNK)
            mixed = jnp.dot(wg, vn[rows, cols], preferred_element_type=F32) + bcol
            o_ref[rows, cols] = (u_ref[rows, cols].astype(F32) * mixed).astype(o_ref.dtype)


def _gmlp(proj, ln_g, ln_b, w_s, b_s_t, tm=512):
    T = proj.shape[0]
    const2 = lambda i: (0, 0)
    return pl.pallas_call(
        _gmlp_kernel,
        grid=(T // tm,),
        in_specs=[pl.BlockSpec((tm, SLOT), lambda i: (i, J_U)),
                  pl.BlockSpec((tm, SLOT), lambda i: (i, J_V)),
                  pl.BlockSpec((1, SLOT), const2),
                  pl.BlockSpec((1, SLOT), const2),
                  pl.BlockSpec((GM_GROUPS, GM_CHUNK, GM_CHUNK), lambda i: (0, 0, 0)),
                  pl.BlockSpec((GM_CHUNK, GM_GROUPS), const2)],
        out_specs=pl.BlockSpec((tm, SLOT), lambda i: (i, 0)),
        out_shape=jax.ShapeDtypeStruct((T, SLOT), BF16),
        compiler_params=_cparams(("parallel",)),
        name="gmlp",
    )(proj, proj, ln_g, ln_b, w_s, b_s_t)


def _qk(q, k):
    return lax.dot_general(q, k, (((1,), (1,)), ((), ())), preferred_element_type=F32)


def _causal(rows, tq):
    r = lax.broadcasted_iota(jnp.int32, (rows, tq), 0) % tq
    c = lax.broadcasted_iota(jnp.int32, (rows, tq), 1)
    return r >= c


def _mask_last_block(s, causal):
    tq = causal.shape[1]
    diag = jnp.where(causal, s[:, -tq:], MASK_VALUE)
    return diag if s.shape[1] == tq else jnp.concatenate([s[:, :-tq], diag], axis=1)


def _row_softmax(s, row_const=None):
    m = jnp.max(s, axis=-1, keepdims=True)
    if row_const is not None:
        m = (m + row_const) - row_const
    p = jnp.exp2(s - m)
    return p, jnp.sum(p, axis=-1, keepdims=True)


def _diff_attn_kernel(q_ref, k_ref, v_ref, c_ref, sa_ref, sb_ref, lam_ref, g_ref, o_ref,
                      qs_ref, kr_ref, *, tq, lambda_init):
    S = q_ref.shape[0]
    c, sa, sb = c_ref[...], sa_ref[...], sb_ref[...]
    qf = _rotate(q_ref[...].astype(F32), c, sa, sb) * (DA_QK_DIM ** -0.5 * LOG2E)
    lane = lax.broadcasted_iota(jnp.int32, qf.shape, 1)
    qs_ref[0] = jnp.where(lane < DA_QK_DIM, qf, 0.0).astype(BF16)
    qs_ref[1] = jnp.where(lane >= DA_QK_DIM, qf, 0.0).astype(BF16)
    kr_ref[...] = _rotate(k_ref[...].astype(F32), c, sa, sb).astype(BF16)

    lp = lam_ref[...]

    def total(x):
        return jnp.sum(jnp.sum(x, axis=1, keepdims=True), axis=0, keepdims=True)

    lam = jnp.exp(total(lp[0:1] * lp[1:2])) - jnp.exp(total(lp[2:3] * lp[3:4])) + lambda_init
    causal = _causal(2 * tq, tq)
    g = g_ref[...]

    def logits(i):
        rows = pl.ds(i * tq, tq)
        q2 = jnp.concatenate([qs_ref[0, rows, :], qs_ref[1, rows, :]], axis=0)
        return _mask_last_block(_qk(q2, kr_ref[pl.ds(0, (i + 1) * tq), :]), causal)

    nq = S // tq
    ahead = logits(0)
    for i in range(nq):
        s = ahead
        if i + 1 < nq:
            ahead = logits(i + 1)
        p, l = _row_softmax(s)
        ratio = lam * l[:tq] / l[tq:]
        o = jnp.dot((p[:tq] - ratio * p[tq:]).astype(BF16), v_ref[pl.ds(0, (i + 1) * tq), :],
                    preferred_element_type=F32) / l[:tq]
        o_ref[pl.ds(i * tq, tq), :] = (_rms(o, g) * (1.0 - lambda_init)).astype(o_ref.dtype)


def _diff_attn(proj, tables, lam, subln_g, B, S, lambda_init, tq=256):
    T = proj.shape[0]
    per_slot = SLOT // HEAD_COLS
    seq = lambda b, h: (b, 0)
    kern = functools.partial(_diff_attn_kernel, tq=tq, lambda_init=lambda_init)
    return pl.pallas_call(
        kern,
        grid=(B, DA_HEADS),
        in_specs=[pl.BlockSpec((S, HEAD_COLS), lambda b, h: (b, J_BQ * per_slot + h)),
                  pl.BlockSpec((S, HEAD_COLS), lambda b, h: (b, J_BK * per_slot + h)),
                  pl.BlockSpec((S, HEAD_COLS), lambda b, h: (b, J_BV * per_slot + h)),
                  pl.BlockSpec((S, LANES), seq),
                  pl.BlockSpec((S, LANES), seq),
                  pl.BlockSpec((S, LANES), seq),
                  pl.BlockSpec(lam.shape, lambda b, h: (0, 0)),
                  pl.BlockSpec((1, HEAD_COLS), lambda b, h: (0, 0))],
        out_specs=pl.BlockSpec((S, HEAD_COLS), lambda b, h: (b, h)),
        out_shape=jax.ShapeDtypeStruct((T, DA_HEADS * HEAD_COLS), BF16),
        scratch_shapes=[pltpu.VMEM((2, S, HEAD_COLS), BF16), pltpu.VMEM((S, HEAD_COLS), BF16)],
        compiler_params=_cparams(("parallel", "parallel")),
        name="diff_attn",
    )(proj, proj, proj, *tables, lam, subln_g)


def _fcum_kernel(f_ref, bf_ref, cum_ref, cumt_ref, *, blk):
    x = f_ref[...] + bf_ref[...]
    logf = jnp.minimum(x, 0.0) - jnp.log1p(jnp.exp(-jnp.abs(x)))
    r = lax.broadcasted_iota(jnp.int32, (blk, blk), 0)
    c = lax.broadcasted_iota(jnp.int32, (blk, blk), 1)
    tri = (r >= c).astype(F32)
    carry = jnp.zeros((1, LANES), F32)
    for n in range(x.shape[0] // blk):
        part = jnp.dot(tri, logf[n * blk:(n + 1) * blk], preferred_element_type=F32,
                       precision=lax.Precision.HIGHEST) + carry
        cum_ref[n * blk:(n + 1) * blk, :] = part
        carry = part[blk - 1:blk, :]
    cum_t = cum_ref[...].T
    for hh in range(FA_HEADS):
        cumt_ref[hh] = cum_t[hh:hh + 1, :]


def _fcum(f_logit, b_f, B, S):
    T = f_logit.shape[0]
    return pl.pallas_call(
        functools.partial(_fcum_kernel, blk=256),
        grid=(B,),
        in_specs=[pl.BlockSpec((S, LANES), lambda b: (b, 0)),
                  pl.BlockSpec((1, LANES), lambda b: (0, 0))],
        out_specs=[pl.BlockSpec((S, LANES), lambda b: (b, 0)),
                   pl.BlockSpec((FA_HEADS, 1, S), lambda b: (b, 0, 0))],
        out_shape=[jax.ShapeDtypeStruct((T, LANES), F32),
                   jax.ShapeDtypeStruct((B * FA_HEADS, 1, S), F32)],
        compiler_params=_cparams(("parallel",)),
        name="forget_cumsum",
    )(f_logit, b_f)


def _fox_attn_kernel(q_ref, k_ref, v_ref, cum_ref, cumt_ref, o_ref, qs_ref, *, tq):
    S = q_ref.shape[0]
    h = pl.program_id(1)
    qs_ref[...] = (q_ref[...].astype(F32) * (FA_HEAD_DIM ** -0.5 * LOG2E)).astype(BF16)
    causal = _causal(tq, tq)

    def logits(i):
        keys = pl.ds(0, (i + 1) * tq)
        s = _qk(qs_ref[pl.ds(i * tq, tq), :], k_ref[keys, :]) - cumt_ref[0, :, keys] * LOG2E
        return _mask_last_block(s, causal)

    nq = S // tq
    ahead = logits(0)
    for i in range(nq):
        rows = pl.ds(i * tq, tq)
        s = ahead
        if i + 1 < nq:
            ahead = logits(i + 1)
        cum = cum_ref[rows, :]
        lane = lax.broadcasted_iota(jnp.int32, cum.shape, 1)
        cq = jnp.sum(jnp.where(lane == h, cum, 0.0), axis=-1, keepdims=True) * LOG2E
        p, l = _row_softmax(s, cq)
        o = jnp.dot(p.astype(BF16), v_ref[pl.ds(0, (i + 1) * tq), :], preferred_element_type=F32)
        o_ref[rows, :] = (o / l).astype(o_ref.dtype)


def _fox_attn(proj, cum, cum_t, B, S, tq=256):
    T = proj.shape[0]
    per_slot = SLOT // HEAD_COLS
    return pl.pallas_call(
        functools.partial(_fox_attn_kernel, tq=tq),
        grid=(B, FA_HEADS),
        in_specs=[pl.BlockSpec((S, HEAD_COLS), lambda b, h: (b, J_CQ * per_slot + h)),
                  pl.BlockSpec((S, HEAD_COLS), lambda b, h: (b, J_CK * per_slot + h)),
                  pl.BlockSpec((S, HEAD_COLS), lambda b, h: (b, J_CV * per_slot + h)),
                  pl.BlockSpec((S, LANES), lambda b, h: (b, 0)),
                  pl.BlockSpec((1, 1, S), lambda b, h: (b * FA_HEADS + h, 0, 0))],
        out_specs=pl.BlockSpec((S, HEAD_COLS), lambda b, h: (b, h)),
        out_shape=jax.ShapeDtypeStruct((T, FA_HEADS * HEAD_COLS), BF16),
        scratch_shapes=[pltpu.VMEM((S, HEAD_COLS), BF16)],
        compiler_params=_cparams(("parallel", "parallel")),
        name="fox_attn",
    )(proj, proj, proj, cum, cum_t)


def _pool_kernel(h_ref, w_ref, sc_ref, o_ref):
    S = h_ref.shape[0]
    t = lax.broadcasted_iota(jnp.int32, (S, HEAD_COLS), 0)
    for g, win in enumerate(POOL_WINDOWS):
        cols = slice(g * HEAD_COLS, (g + 1) * HEAD_COLS)
        x = h_ref[:, cols].astype(F32)
        tot = x
        span = 1
        while span < win:
            tot = tot + jnp.where(t >= span, pltpu.roll(tot, span, 0), 0.0)
            span *= 2
        cnt = jnp.minimum(t + 1, win).astype(F32)
        pooled = (tot / cnt - x).astype(BF16)
        y = jnp.dot(pooled, w_ref[g].astype(BF16), preferred_element_type=F32)
        o_ref[:, cols] = (y * sc_ref[:, cols]).astype(o_ref.dtype)


def _pool(proj, w_pool, scale, B, S):
    T = proj.shape[0]
    return pl.pallas_call(
        _pool_kernel,
        grid=(B,),
        in_specs=[pl.BlockSpec((S, SLOT), lambda b: (b, J_DH)),
                  pl.BlockSpec(w_pool.shape, lambda b: (0, 0, 0)),
                  pl.BlockSpec((1, SLOT), lambda b: (0, 0))],
        out_specs=pl.BlockSpec((S, SLOT), lambda b: (b, 0)),
        out_shape=jax.ShapeDtypeStruct((T, SLOT), BF16),
        compiler_params=_cparams(("parallel",)),
        name="pool",
    )(proj, w_pool, scale)


def _merge_kernel(oa_ref, ob_ref, oc_ref, od_ref, gate_ref, wb_ref, wo_ref, g_ref, h_ref, out_ref):
    D = h_ref.shape[1]
    merged = None
    for n, br in enumerate((oa_ref, ob_ref, oc_ref, od_ref)):
        bd = jnp.dot(br[...], wb_ref[n], preferred_element_type=F32)
        term = jax.nn.sigmoid(gate_ref[:, n * D:(n + 1) * D].astype(F32)) * bd
        merged = term if merged is None else merged + term
    y = jnp.dot(merged.astype(BF16), wo_ref[...], preferred_element_type=F32)
    out_ref[...] = h_ref[...] + _rms(y, g_ref[...])


def _merge(branches, gates, w_branch, w_out, l, g, h, tm=256):
    T, D = h.shape
    row = lambda i: (i, 0)
    once = pl.Buffered(1)
    return pl.pallas_call(
        _merge_kernel,
        grid=(T // tm,),
        in_specs=[pl.BlockSpec((tm, SLOT), row)] * N_BRANCH + [
            pl.BlockSpec((tm, N_BRANCH * D), row),
            pl.BlockSpec((None,) + w_branch.shape[1:], lambda i: (l, 0, 0, 0), pipeline_mode=once),
            pl.BlockSpec((None,) + w_out.shape[1:], lambda i: (l, 0, 0), pipeline_mode=once),
            pl.BlockSpec((1, D), lambda i: (0, 0)),
            pl.BlockSpec((tm, D), row)],
        out_specs=pl.BlockSpec((tm, D), row),
        out_shape=jax.ShapeDtypeStruct((T, D), F32),
        compiler_params=_cparams(("parallel",), 48),
        name="merge_out",
    )(*branches, gates, w_branch, w_out, g, h)


def _ffn_kernel(h_ref, g1_ref, wu_ref, wd_ref, g2_ref, out_ref, hn_ref, acc_ref):
    f = pl.program_id(1)

    @pl.when(f == 0)
    def _():
        hn_ref[...] = _rms(h_ref[...], g1_ref[...]).astype(BF16)
        acc_ref[...] = jnp.zeros_like(acc_ref)

    up = jnp.dot(hn_ref[...], wu_ref[...], preferred_element_type=F32)
    a = jnp.square(jnp.maximum(up, 0.0)).astype(BF16)
    acc_ref[...] += jnp.dot(a, wd_ref[...], preferred_element_type=F32)

    @pl.when(f == pl.num_programs(1) - 1)
    def _():
        out_ref[...] = h_ref[...] + _rms(acc_ref[...], g2_ref[...])


def _ffn(h, g1, w_up, w_down, l, g2, tm=512, tf=1024):
    T, D = h.shape
    F = w_up.shape[2]
    row = lambda i, f: (i, 0)
    vec = lambda i, f: (0, 0)
    return pl.pallas_call(
        _ffn_kernel,
        grid=(T // tm, F // tf),
        in_specs=[pl.BlockSpec((tm, D), row),
                  pl.BlockSpec((1, D), vec),
                  pl.BlockSpec((None, D, tf), lambda i, f: (l, 0, f)),
                  pl.BlockSpec((None, tf, D), lambda i, f: (l, f, 0)),
                  pl.BlockSpec((1, D), vec)],
        out_specs=pl.BlockSpec((tm, D), row),
        out_shape=jax.ShapeDtypeStruct((T, D), F32),
        scratch_shapes=[pltpu.VMEM((tm, D), BF16), pltpu.VMEM((tm, D), F32)],
        compiler_params=_cparams(("parallel", "arbitrary"), 48),
        name="ffn",
    )(h, g1, w_up, w_down, g2)


def kernel(x, positions, norm_mix_pre, norm_mix_post, norm_ffn_pre, norm_ffn_post, w_in, gm_ln_g,
           gm_ln_b, gm_w_s, gm_b_s, da_lambda, da_subln_g, fa_b_f, pool_w, pool_scale, w_branch,
           w_out, w_ffn_up, w_ffn_down):
    B, S, D = x.shape
    T = B * S
    h = x.reshape(T, D)
    tables = _rope_tables(positions)
    w_t, wf_t = _prep_w_in(w_in)
    w_branch, w_out, w_ffn_up, w_ffn_down = (w.astype(BF16) for w in (w_branch, w_out, w_ffn_up, w_ffn_down))
    row = lambda a: a.reshape(1, -1)
    for l in range(DEPTH):
        lambda_init = 0.8 - 0.6 * math.exp(-0.3 * l)
        proj, f_logit = _inproj(h, row(norm_mix_pre[l]), w_t, wf_t, l)

        o_a = _gmlp(proj, row(gm_ln_g[l]), row(gm_ln_b[l]), gm_w_s[l], gm_b_s[l].T)
        o_b = _diff_attn(proj, tables, da_lambda[l], row(da_subln_g[l]), B, S, lambda_init)
        b_f = jnp.pad(fa_b_f[l], (0, LANES - FA_HEADS)).reshape(1, LANES)
        cum, cum_t = _fcum(f_logit, b_f, B, S)
        o_c = _fox_attn(proj, cum, cum_t, B, S)
        o_d = _pool(proj, pool_w[l], row(pool_scale[l]), B, S)

        h = _merge((o_a, o_b, o_c, o_d), proj, w_branch, w_out, l, row(norm_mix_post[l]), h)
        h = _ffn(h, row(norm_ffn_pre[l]), w_ffn_up, w_ffn_down, l, row(norm_ffn_post[l]))
    return h.reshape(B, S, D)
```

```python
import functools
import math

import jax
import jax.numpy as jnp
from jax import lax
from jax.experimental import pallas as pl
from jax.experimental.pallas import tpu as pltpu

F32 = jnp.float32
BF16 = jnp.bfloat16

DEPTH = 2
N_BRANCH = 4
GM_GROUPS = 4
GM_CHUNK = 128
DA_HEADS = 4
DA_QK_DIM = 64
DA_ROT_DIM = 16
FA_HEADS = 4
FA_HEAD_DIM = 128
POOL_WINDOWS = (2, 4, 8, 16)
ROPE_THETA = 500000.0
NORM_EPS = 1e-6
MASK_VALUE = -1e30
LOG2E = math.log2(math.e)

LANES = 128
SUBLANES = 8
ROW_CHUNK = 2 * SUBLANES
CAST_BLOCKS = 64
HEAD_COLS = 128
SLOT = 512

N_GATE_SLOTS = 16
J_U, J_V, J_BQ, J_BK, J_BV, J_CQ, J_CK, J_CV, J_DH = range(N_GATE_SLOTS, N_GATE_SLOTS + 9)
N_SLOTS = J_DH + 1
N_ALIGNED = 8


def _rms(x, g):
    return x * lax.rsqrt(jnp.mean(x * x, axis=-1, keepdims=True) + NORM_EPS) * g


def _row_chunks(n_rows, rows=ROW_CHUNK):
    return [pl.ds(r0, rows) for r0 in range(0, n_rows, rows)]


def _cparams(sem, vmem_mb=None):
    kw = dict(dimension_semantics=sem)
    if vmem_mb is not None:
        kw["vmem_limit_bytes"] = vmem_mb * 1024 * 1024
    return pltpu.CompilerParams(**kw)


def _rope_kernel(pos_ref, inv_ref, c_ref, sa_ref, sb_ref):
    ang = pos_ref[...].astype(F32) * inv_ref[...]
    sub = lax.broadcasted_iota(jnp.int32, ang.shape, 1) % DA_QK_DIM
    half = DA_ROT_DIM // 2
    s = jnp.sin(ang)
    c_ref[...] = jnp.cos(ang)
    sa_ref[...] = jnp.where(sub < half, -s, 0.0)
    sb_ref[...] = jnp.where((sub >= half) & (sub < 2 * half), s, 0.0)


def _rope_tables(positions):
    T = positions.size
    tm = 1024
    half = DA_ROT_DIM // 2
    inv = 1.0 / (ROPE_THETA ** (jnp.arange(0, DA_ROT_DIM, 2, dtype=F32) / DA_ROT_DIM))
    sub = jnp.arange(LANES) % DA_QK_DIM
    inv_lane = jnp.where(sub < 2 * half, inv[sub % half], 0.0).reshape(1, LANES)
    tab = jax.ShapeDtypeStruct((T, LANES), F32)
    return pl.pallas_call(
        _rope_kernel,
        grid=(T // tm,),
        in_specs=[pl.BlockSpec((tm, 1), lambda i: (i, 0)),
                  pl.BlockSpec((1, LANES), lambda i: (0, 0))],
        out_specs=[pl.BlockSpec((tm, LANES), lambda i: (i, 0))] * 3,
        out_shape=[tab] * 3,
        compiler_params=_cparams(("parallel",)),
        name="rope_tables",
    )(positions.reshape(T, 1), inv_lane)


def _rotate(x, c, sa, sb):
    half = DA_ROT_DIM // 2
    return x * c + pltpu.roll(x, LANES - half, 1) * sa + pltpu.roll(x, half, 1) * sb


def _wprep_kernel(a_ref, o_ref):
    n_layers, tn, _ = o_ref.shape
    sub = lax.broadcasted_iota(jnp.int32, (SUBLANES, LANES), 0)
    low = {d: (sub & d) == 0 for d in (4, 2, 1)}

    def transpose8(a):
        for d in (4, 2, 1):
            nxt = list(a)
            for i in range(SUBLANES):
                if not i & d:
                    x, y = a[i], a[i + d]
                    nxt[i] = jnp.where(low[d], x, pltpu.roll(y, d, 0))
                    nxt[i + d] = jnp.where(low[d], pltpu.roll(x, SUBLANES - d, 0), y)
            a = nxt
        return a

    rows_per_iter = 2 * SUBLANES

    def body(it, carry):
        n0 = pl.multiple_of(it * rows_per_iter, rows_per_iter)
        for jb in range(a_ref.shape[1] // SUBLANES):
            js = pl.ds(jb * SUBLANES, SUBLANES)
            halves = [transpose8([a_ref[n0 + half * SUBLANES + i, js, :] for i in range(SUBLANES)])
                      for half in range(2)]
            for s in range(SUBLANES):
                kb, l = divmod(jb * SUBLANES + s, n_layers)
                tile = jnp.concatenate([halves[0][s], halves[1][s]], axis=0)
                o_ref[l, pl.ds(n0, rows_per_iter), kb * LANES:(kb + 1) * LANES] = tile.astype(o_ref.dtype)
        return carry

    lax.fori_loop(0, tn // rows_per_iter, body, 0)


def _native_rows(w_in):
    L, D, N = w_in.shape
    v = jnp.transpose(w_in.reshape(L, D // LANES, LANES, N), (3, 1, 0, 2))
    return v.reshape(N, (D // LANES) * L, LANES)


def _prep_rows(w_rows, L, D, tn, n, offset):
    return pl.pallas_call(
        _wprep_kernel,
        grid=(n,),
        in_specs=[pl.BlockSpec((pl.Element(tn), pl.Element(w_rows.shape[1]), pl.Element(LANES)),
                               lambda j: (offset(j), 0, 0))],
        out_specs=pl.BlockSpec((L, tn, D), lambda j: (0, j, 0)),
        out_shape=jax.ShapeDtypeStruct((L, n * tn, D), BF16),
        compiler_params=_cparams(("parallel",), 40),
        name="w_in_prep",
    )(w_rows)


def _prep_w_in(w_in):
    L, D, _ = w_in.shape
    f0 = N_ALIGNED * SLOT
    d0 = f0 + FA_HEADS
    g0 = d0 + SLOT

    def slot_row(j):
        return jnp.where(j < N_GATE_SLOTS, g0 + SLOT * j, jnp.where(j == J_DH, d0, SLOT * (j - N_GATE_SLOTS)))

    w_rows = _native_rows(w_in)
    return (_prep_rows(w_rows, L, D, SLOT, N_SLOTS, slot_row),
            _prep_rows(w_rows, L, D, LANES, 1, lambda j: f0 + 0 * j))


def _dot_nt(a, b):
    return lax.dot_general(a, b, (((1,), (1,)), ((), ())), preferred_element_type=F32)


def _inproj_kernel(*refs, n_cast):
    h_ref, g_ref, w_ref, wf_ref = refs[:4]
    cast_in = refs[4:4 + n_cast]
    out_ref, f_ref = refs[4 + n_cast:6 + n_cast]
    cast_out = refs[6 + n_cast:6 + 2 * n_cast]
    xn_ref = refs[-1]

    @pl.when(pl.program_id(1) == 0)
    def _():
        xn_ref[...] = _rms(h_ref[...], g_ref[...]).astype(BF16)
        f_ref[...] = _dot_nt(xn_ref[...], wf_ref[...])

    out_ref[...] = _dot_nt(xn_ref[...], w_ref[...]).astype(out_ref.dtype)
    for src, dst in zip(cast_in, cast_out):
        dst[...] = src[...].astype(dst.dtype)


def _inproj(h, g, w_t, wf_t, l, to_cast, tm=1024, tn=1280):
    T, D = h.shape
    N = w_t.shape[1]
    ni, nj = T // tm, N // tn
    n_blk = CAST_BLOCKS
    assert ni * nj >= n_blk
    row = lambda i, j: (i, 0)
    blk = lambda i, j: jnp.minimum(i * nj + j, n_blk - 1)
    cast_in, cast_out, cast_shape = [], [], []
    for w in to_cast:
        _, R, C = w.shape
        cast_in.append(pl.BlockSpec((None, R // n_blk, C), lambda i, j: (l, blk(i, j), 0)))
        cast_out.append(pl.BlockSpec((R // n_blk, C), lambda i, j: (blk(i, j), 0)))
        cast_shape.append(jax.ShapeDtypeStruct((R, C), BF16))
    res = pl.pallas_call(
        functools.partial(_inproj_kernel, n_cast=len(to_cast)),
        grid=(ni, nj),
        in_specs=[pl.BlockSpec((tm, D), row),
                  pl.BlockSpec((1, D), lambda i, j: (0, 0)),
                  pl.BlockSpec((None, tn, D), lambda i, j: (l, j, 0)),
                  pl.BlockSpec((None, LANES, D), lambda i, j: (l, 0, 0))] + cast_in,
        out_specs=[pl.BlockSpec((tm, tn), lambda i, j: (i, j)),
                   pl.BlockSpec((tm, LANES), row)] + cast_out,
        out_shape=[jax.ShapeDtypeStruct((T, N), BF16),
                   jax.ShapeDtypeStruct((T, LANES), F32)] + cast_shape,
        scratch_shapes=[pltpu.VMEM((tm, D), BF16)],
        compiler_params=_cparams(("arbitrary", "arbitrary"), 56),
        name="inproj",
    )(h, g, w_t, wf_t, *to_cast)
    return res[0], res[1], res[2:]


def _gmlp_kernel(u_ref, v_ref, lng_ref, lnb_ref, ws_ref, bst_ref, o_ref):
    v = v_ref[...].astype(F32)
    mu = jnp.mean(v, axis=-1, keepdims=True)
    vc = v - mu
    var = jnp.mean(vc * vc, axis=-1, keepdims=True)
    vn = (vc * lax.rsqrt(var + NORM_EPS) * lng_ref[...] + lnb_ref[...]).astype(BF16)
    r = lax.broadcasted_iota(jnp.int32, (GM_CHUNK, GM_CHUNK), 0)
    c = lax.broadcasted_iota(jnp.int32, (GM_CHUNK, GM_CHUNK), 1)
    causal = r >= c
    for g in range(GM_GROUPS):
        wg = jnp.where(causal, ws_ref[g], 0.0).astype(BF16)
        bcol = bst_ref[:, g:g + 1]
        cols = slice(g * HEAD_COLS, (g + 1) * HEAD_COLS)
        for n in range(v.shape[0] // GM_CHUNK):
            rows = slice(n * GM_CHUNK, (n + 1) * GM_CHUNK)
            mixed = jnp.dot(wg, vn[rows, cols], preferred_element_type=F32) + bcol
            o_ref[rows, cols] = (u_ref[rows, cols].astype(F32) * mixed).astype(o_ref.dtype)


def _gmlp(proj, ln_g, ln_b, w_s, b_s_t, tm=512):
    T = proj.shape[0]
    const2 = lambda i: (0, 0)
    return pl.pallas_call(
        _gmlp_kernel,
        grid=(T // tm,),
        in_specs=[pl.BlockSpec((tm, SLOT), lambda i: (i, J_U)),
                  pl.BlockSpec((tm, SLOT), lambda i: (i, J_V)),
                  pl.BlockSpec((1, SLOT), const2),
                  pl.BlockSpec((1, SLOT), const2),
                  pl.BlockSpec((GM_GROUPS, GM_CHUNK, GM_CHUNK), lambda i: (0, 0, 0)),
                  pl.BlockSpec((GM_CHUNK, GM_GROUPS), const2)],
        out_specs=pl.BlockSpec((tm, SLOT), lambda i: (i, 0)),
        out_shape=jax.ShapeDtypeStruct((T, SLOT), BF16),
        compiler_params=_cparams(("parallel",)),
        name="gmlp",
    )(proj, proj, ln_g, ln_b, w_s, b_s_t)


def _qk(q, k):
    return lax.dot_general(q, k, (((1,), (1,)), ((), ())), preferred_element_type=F32)


def _causal(rows, tq):
    r = lax.broadcasted_iota(jnp.int32, (rows, tq), 0) % tq
    c = lax.broadcasted_iota(jnp.int32, (rows, tq), 1)
    return r >= c


def _mask_last_block(s, causal):
    tq = causal.shape[1]
    diag = jnp.where(causal, s[:, -tq:], MASK_VALUE)
    return diag if s.shape[1] == tq else jnp.concatenate([s[:, :-tq], diag], axis=1)


def _row_softmax(s, row_const=None):
    m = jnp.max(s, axis=-1, keepdims=True)
    if row_const is not None:
        m = (m + row_const) - row_const
    p = jnp.exp2(s - m)
    return p, jnp.sum(p, axis=-1, keepdims=True)


def _diff_attn_kernel(q_ref, k_ref, v_ref, c_ref, sa_ref, sb_ref, lam_ref, g_ref, o_ref,
                      qs_ref, kr_ref, *, tq, lambda_init):
    S = q_ref.shape[0]
    c, sa, sb = c_ref[...], sa_ref[...], sb_ref[...]
    qf = _rotate(q_ref[...].astype(F32), c, sa, sb) * (DA_QK_DIM ** -0.5 * LOG2E)
    lane = lax.broadcasted_iota(jnp.int32, qf.shape, 1)
    qs_ref[0] = jnp.where(lane < DA_QK_DIM, qf, 0.0).astype(BF16)
    qs_ref[1] = jnp.where(lane >= DA_QK_DIM, qf, 0.0).astype(BF16)
    kr_ref[...] = _rotate(k_ref[...].astype(F32), c, sa, sb).astype(BF16)

    lp = lam_ref[...]

    def total(x):
        return jnp.sum(jnp.sum(x, axis=1, keepdims=True), axis=0, keepdims=True)

    lam = jnp.exp(total(lp[0:1] * lp[1:2])) - jnp.exp(total(lp[2:3] * lp[3:4])) + lambda_init
    causal = _causal(2 * tq, tq)
    g = g_ref[...]

    def logits(i):
        rows = pl.ds(i * tq, tq)
        q2 = jnp.concatenate([qs_ref[0, rows, :], qs_ref[1, rows, :]], axis=0)
        return _mask_last_block(_qk(q2, kr_ref[pl.ds(0, (i + 1) * tq), :]), causal)

    nq = S // tq
    ahead = logits(0)
    for i in range(nq):
        s = ahead
        if i + 1 < nq:
            ahead = logits(i + 1)
        p, l = _row_softmax(s)
        ratio = lam * l[:tq] / l[tq:]
        o = jnp.dot((p[:tq] - ratio * p[tq:]).astype(BF16), v_ref[pl.ds(0, (i + 1) * tq), :],
                    preferred_element_type=F32) / l[:tq]
        o_ref[pl.ds(i * tq, tq), :] = (_rms(o, g) * (1.0 - lambda_init)).astype(o_ref.dtype)


def _diff_attn(proj, tables, lam, subln_g, B, S, lambda_init, tq=256):
    T = proj.shape[0]
    per_slot = SLOT // HEAD_COLS
    seq = lambda b, h: (b, 0)
    kern = functools.partial(_diff_attn_kernel, tq=tq, lambda_init=lambda_init)
    return pl.pallas_call(
        kern,
        grid=(B, DA_HEADS),
        in_specs=[pl.BlockSpec((S, HEAD_COLS), lambda b, h: (b, J_BQ * per_slot + h)),
                  pl.BlockSpec((S, HEAD_COLS), lambda b, h: (b, J_BK * per_slot + h)),
                  pl.BlockSpec((S, HEAD_COLS), lambda b, h: (b, J_BV * per_slot + h)),
                  pl.BlockSpec((S, LANES), seq),
                  pl.BlockSpec((S, LANES), seq),
                  pl.BlockSpec((S, LANES), seq),
                  pl.BlockSpec(lam.shape, lambda b, h: (0, 0)),
                  pl.BlockSpec((1, HEAD_COLS), lambda b, h: (0, 0))],
        out_specs=pl.BlockSpec((S, HEAD_COLS), lambda b, h: (b, h)),
        out_shape=jax.ShapeDtypeStruct((T, DA_HEADS * HEAD_COLS), BF16),
        scratch_shapes=[pltpu.VMEM((2, S, HEAD_COLS), BF16), pltpu.VMEM((S, HEAD_COLS), BF16)],
        compiler_params=_cparams(("parallel", "parallel")),
        name="diff_attn",
    )(proj, proj, proj, *tables, lam, subln_g)


def _fcum_kernel(f_ref, bf_ref, cum_ref, cumt_ref, *, blk):
    x = f_ref[...] + bf_ref[...]
    logf = jnp.minimum(x, 0.0) - jnp.log1p(jnp.exp(-jnp.abs(x)))
    r = lax.broadcasted_iota(jnp.int32, (blk, blk), 0)
    c = lax.broadcasted_iota(jnp.int32, (blk, blk), 1)
    tri = (r >= c).astype(F32)
    carry = jnp.zeros((1, LANES), F32)
    for n in range(x.shape[0] // blk):
        part = jnp.dot(tri, logf[n * blk:(n + 1) * blk], preferred_element_type=F32,
                       precision=lax.Precision.HIGHEST) + carry
        cum_ref[n * blk:(n + 1) * blk, :] = part
        carry = part[blk - 1:blk, :]
    cum_t = cum_ref[...].T
    for hh in range(FA_HEADS):
        cumt_ref[hh] = cum_t[hh:hh + 1, :]


def _fcum(f_logit, b_f, B, S):
    T = f_logit.shape[0]
    return pl.pallas_call(
        functools.partial(_fcum_kernel, blk=256),
        grid=(B,),
        in_specs=[pl.BlockSpec((S, LANES), lambda b: (b, 0)),
                  pl.BlockSpec((1, LANES), lambda b: (0, 0))],
        out_specs=[pl.BlockSpec((S, LANES), lambda b: (b, 0)),
                   pl.BlockSpec((FA_HEADS, 1, S), lambda b: (b, 0, 0))],
        out_shape=[jax.ShapeDtypeStruct((T, LANES), F32),
                   jax.ShapeDtypeStruct((B * FA_HEADS, 1, S), F32)],
        compiler_params=_cparams(("parallel",)),
        name="forget_cumsum",
    )(f_logit, b_f)


def _fox_attn_kernel(q_ref, k_ref, v_ref, cum_ref, cumt_ref, o_ref, qs_ref, *, tq):
    S = q_ref.shape[0]
    h = pl.program_id(1)
    qs_ref[...] = (q_ref[...].astype(F32) * (FA_HEAD_DIM ** -0.5 * LOG2E)).astype(BF16)
    causal = _causal(tq, tq)

    def logits(i):
        keys = pl.ds(0, (i + 1) * tq)
        s = _qk(qs_ref[pl.ds(i * tq, tq), :], k_ref[keys, :]) - cumt_ref[0, :, keys] * LOG2E
        return _mask_last_block(s, causal)

    nq = S // tq
    ahead = logits(0)
    for i in range(nq):
        rows = pl.ds(i * tq, tq)
        s = ahead
        if i + 1 < nq:
            ahead = logits(i + 1)
        cum = cum_ref[rows, :]
        lane = lax.broadcasted_iota(jnp.int32, cum.shape, 1)
        cq = jnp.sum(jnp.where(lane == h, cum, 0.0), axis=-1, keepdims=True) * LOG2E
        p, l = _row_softmax(s, cq)
        o = jnp.dot(p.astype(BF16), v_ref[pl.ds(0, (i + 1) * tq), :], preferred_element_type=F32)
        o_ref[rows, :] = (o / l).astype(o_ref.dtype)


def _fox_attn(proj, cum, cum_t, B, S, tq=256):
    T = proj.shape[0]
    per_slot = SLOT // HEAD_COLS
    return pl.pallas_call(
        functools.partial(_fox_attn_kernel, tq=tq),
        grid=(B, FA_HEADS),
        in_specs=[pl.BlockSpec((S, HEAD_COLS), lambda b, h: (b, J_CQ * per_slot + h)),
                  pl.BlockSpec((S, HEAD_COLS), lambda b, h: (b, J_CK * per_slot + h)),
                  pl.BlockSpec((S, HEAD_COLS), lambda b, h: (b, J_CV * per_slot + h)),
                  pl.BlockSpec((S, LANES), lambda b, h: (b, 0)),
                  pl.BlockSpec((1, 1, S), lambda b, h: (b * FA_HEADS + h, 0, 0))],
        out_specs=pl.BlockSpec((S, HEAD_COLS), lambda b, h: (b, h)),
        out_shape=jax.ShapeDtypeStruct((T, FA_HEADS * HEAD_COLS), BF16),
        scratch_shapes=[pltpu.VMEM((S, HEAD_COLS), BF16)],
        compiler_params=_cparams(("parallel", "parallel")),
        name="fox_attn",
    )(proj, proj, proj, cum, cum_t)


def _pool_kernel(h_ref, w_ref, sc_ref, o_ref):
    S = h_ref.shape[0]
    t = lax.broadcasted_iota(jnp.int32, (S, HEAD_COLS), 0)
    for g, win in enumerate(POOL_WINDOWS):
        cols = slice(g * HEAD_COLS, (g + 1) * HEAD_COLS)
        x = h_ref[:, cols].astype(F32)
        tot = x
        span = 1
        while span < win:
            tot = tot + jnp.where(t >= span, pltpu.roll(tot, span, 0), 0.0)
            span *= 2
        cnt = jnp.minimum(t + 1, win).astype(F32)
        pooled = (tot / cnt - x).astype(BF16)
        y = jnp.dot(pooled, w_ref[g].astype(BF16), preferred_element_type=F32)
        o_ref[:, cols] = (y * sc_ref[:, cols]).astype(o_ref.dtype)


def _pool(proj, w_pool, scale, B, S):
    T = proj.shape[0]
    return pl.pallas_call(
        _pool_kernel,
        grid=(B,),
        in_specs=[pl.BlockSpec((S, SLOT), lambda b: (b, J_DH)),
                  pl.BlockSpec(w_pool.shape, lambda b: (0, 0, 0)),
                  pl.BlockSpec((1, SLOT), lambda b: (0, 0))],
        out_specs=pl.BlockSpec((S, SLOT), lambda b: (b, 0)),
        out_shape=jax.ShapeDtypeStruct((T, SLOT), BF16),
        compiler_params=_cparams(("parallel",)),
        name="pool",
    )(proj, w_pool, scale)


def _merge_kernel(oa_ref, ob_ref, oc_ref, od_ref, gate_ref, wb_ref, wo_ref, g_ref, h_ref, out_ref):
    D = h_ref.shape[1]
    merged = None
    for n, br in enumerate((oa_ref, ob_ref, oc_ref, od_ref)):
        bd = jnp.dot(br[...], wb_ref[n], preferred_element_type=F32)
        term = jax.nn.sigmoid(gate_ref[:, n * D:(n + 1) * D].astype(F32)) * bd
        merged = term if merged is None else merged + term
    y = jnp.dot(merged.astype(BF16), wo_ref[...], preferred_element_type=F32)
    out_ref[...] = h_ref[...] + _rms(y, g_ref[...])


def _merge(branches, gates, w_branch, w_out, g, h, tm=256):
    T, D = h.shape
    row = lambda i: (i, 0)
    once = pl.Buffered(1)
    return pl.pallas_call(
        _merge_kernel,
        grid=(T // tm,),
        in_specs=[pl.BlockSpec((tm, SLOT), row)] * N_BRANCH + [
            pl.BlockSpec((tm, N_BRANCH * D), row),
            pl.BlockSpec(w_branch.shape, lambda i: (0, 0, 0), pipeline_mode=once),
            pl.BlockSpec(w_out.shape, lambda i: (0, 0), pipeline_mode=once),
            pl.BlockSpec((1, D), lambda i: (0, 0)),
            pl.BlockSpec((tm, D), row)],
        out_specs=pl.BlockSpec((tm, D), row),
        out_shape=jax.ShapeDtypeStruct((T, D), F32),
        compiler_params=_cparams(("parallel",), 48),
        name="merge_out",
    )(*branches, gates, w_branch, w_out, g, h)


def _ffn_kernel(h_ref, g1_ref, wu_ref, wd_ref, g2_ref, out_ref, hn_ref, acc_ref):
    f = pl.program_id(1)

    @pl.when(f == 0)
    def _():
        hn_ref[...] = _rms(h_ref[...], g1_ref[...]).astype(BF16)
        acc_ref[...] = jnp.zeros_like(acc_ref)

    up = jnp.dot(hn_ref[...], wu_ref[...], preferred_element_type=F32)
    a = jnp.square(jnp.maximum(up, 0.0)).astype(BF16)
    acc_ref[...] += jnp.dot(a, wd_ref[...], preferred_element_type=F32)

    @pl.when(f == pl.num_programs(1) - 1)
    def _():
        g2 = g2_ref[...]
        for r in _row_chunks(h_ref.shape[0]):
            out_ref[r, :] = h_ref[r, :] + _rms(acc_ref[r, :], g2)


def _ffn(h, g1, w_up, w_down, g2, tm=512, tf=1024):
    T, D = h.shape
    F = w_up.shape[1]
    row = lambda i, f: (i, 0)
    vec = lambda i, f: (0, 0)
    return pl.pallas_call(
        _ffn_kernel,
        grid=(T // tm, F // tf),
        in_specs=[pl.BlockSpec((tm, D), row),
                  pl.BlockSpec((1, D), vec),
                  pl.BlockSpec((D, tf), lambda i, f: (0, f)),
                  pl.BlockSpec((tf, D), lambda i, f: (f, 0)),
                  pl.BlockSpec((1, D), vec)],
        out_specs=pl.BlockSpec((tm, D), row),
        out_shape=jax.ShapeDtypeStruct((T, D), F32),
        scratch_shapes=[pltpu.VMEM((tm, D), BF16), pltpu.VMEM((tm, D), F32)],
        compiler_params=_cparams(("parallel", "arbitrary"), 48),
        name="ffn",
    )(h, g1, w_up, w_down, g2)


def kernel(x, positions, norm_mix_pre, norm_mix_post, norm_ffn_pre, norm_ffn_post, w_in, gm_ln_g,
           gm_ln_b, gm_w_s, gm_b_s, da_lambda, da_subln_g, fa_b_f, pool_w, pool_scale, w_branch,
           w_out, w_ffn_up, w_ffn_down):
    B, S, D = x.shape
    T = B * S
    h = x.reshape(T, D)
    tables = _rope_tables(positions)
    w_t, wf_t = _prep_w_in(w_in)
    L, NB, W, _ = w_branch.shape
    to_cast = (w_branch.reshape(L, NB * W, D), w_out, w_ffn_up, w_ffn_down)
    row = lambda a: a.reshape(1, -1)
    for l in range(DEPTH):
        lambda_init = 0.8 - 0.6 * math.exp(-0.3 * l)
        proj, f_logit, (wb, wo, wu, wd) = _inproj(h, row(norm_mix_pre[l]), w_t, wf_t, l, to_cast)

        o_a = _gmlp(proj, row(gm_ln_g[l]), row(gm_ln_b[l]), gm_w_s[l], gm_b_s[l].T)
        o_b = _diff_attn(proj, tables, da_lambda[l], row(da_subln_g[l]), B, S, lambda_init)
        b_f = jnp.pad(fa_b_f[l], (0, LANES - FA_HEADS)).reshape(1, LANES)
        cum, cum_t = _fcum(f_logit, b_f, B, S)
        o_c = _fox_attn(proj, cum, cum_t, B, S)
        o_d = _pool(proj, pool_w[l], row(pool_scale[l]), B, S)

        h = _merge((o_a, o_b, o_c, o_d), proj, wb.reshape(NB, W, D), wo, row(norm_mix_post[l]), h)
        h = _ffn(h, row(norm_ffn_pre[l]), wu, wd, row(norm_ffn_post[l]))
    return h.reshape(B, S, D)
```

```python
import functools
import math

import jax
import jax.numpy as jnp
from jax import lax
from jax.experimental import pallas as pl
from jax.experimental.pallas import tpu as pltpu

F32 = jnp.float32
BF16 = jnp.bfloat16

DEPTH = 2
N_BRANCH = 4
GM_GROUPS = 4
GM_CHUNK = 128
DA_HEADS = 4
DA_QK_DIM = 64
DA_ROT_DIM = 16
FA_HEADS = 4
FA_HEAD_DIM = 128
POOL_WINDOWS = (2, 4, 8, 16)
ROPE_THETA = 500000.0
NORM_EPS = 1e-6
MASK_VALUE = -1e30
LOG2E = math.log2(math.e)

LANES = 128
SUBLANES = 8
ROW_CHUNK = 2 * SUBLANES
CAST_BLOCKS = 64
HEAD_COLS = 128
SLOT = 512
SLOT_GROUPS = SLOT // HEAD_COLS


N_GATE_SLOTS = 16
J_U, J_V, J_BQ, J_BK, J_BV, J_CQ, J_CK, J_CV, J_DH = range(N_GATE_SLOTS, N_GATE_SLOTS + 9)
N_SLOTS = J_DH + 1
N_ALIGNED = 8


def _rms(x, g):
    return x * lax.rsqrt(jnp.mean(x * x, axis=-1, keepdims=True) + NORM_EPS) * g


def _row_chunks(n_rows, rows=ROW_CHUNK):
    return [pl.ds(r0, rows) for r0 in range(0, n_rows, rows)]


def _cparams(sem, vmem_mb=None):
    kw = dict(dimension_semantics=sem)
    if vmem_mb is not None:
        kw["vmem_limit_bytes"] = vmem_mb * 1024 * 1024
    return pltpu.CompilerParams(**kw)


def _rope_kernel(pos_ref, inv_ref, c_ref, sa_ref, sb_ref):
    ang = pos_ref[...].astype(F32) * inv_ref[...]
    sub = lax.broadcasted_iota(jnp.int32, ang.shape, 1) % DA_QK_DIM
    half = DA_ROT_DIM // 2
    s = jnp.sin(ang)
    c_ref[...] = jnp.cos(ang)
    sa_ref[...] = jnp.where(sub < half, -s, 0.0)
    sb_ref[...] = jnp.where((sub >= half) & (sub < 2 * half), s, 0.0)


def _rope_tables(positions):
    T = positions.size
    tm = 1024
    half = DA_ROT_DIM // 2
    inv = 1.0 / (ROPE_THETA ** (jnp.arange(0, DA_ROT_DIM, 2, dtype=F32) / DA_ROT_DIM))
    sub = jnp.arange(LANES) % DA_QK_DIM
    inv_lane = jnp.where(sub < 2 * half, inv[sub % half], 0.0).reshape(1, LANES)
    tab = jax.ShapeDtypeStruct((T, LANES), F32)
    return pl.pallas_call(
        _rope_kernel,
        grid=(T // tm,),
        in_specs=[pl.BlockSpec((tm, 1), lambda i: (i, 0)),
                  pl.BlockSpec((1, LANES), lambda i: (0, 0))],
        out_specs=[pl.BlockSpec((tm, LANES), lambda i: (i, 0))] * 3,
        out_shape=[tab] * 3,
        compiler_params=_cparams(("parallel",)),
        name="rope_tables",
    )(positions.reshape(T, 1), inv_lane)


def _rotate(x, c, sa, sb):
    half = DA_ROT_DIM // 2
    return x * c + pltpu.roll(x, LANES - half, 1) * sa + pltpu.roll(x, half, 1) * sb


def _wprep_kernel(a_ref, o_ref):
    n_layers, tn, _ = o_ref.shape
    sub = lax.broadcasted_iota(jnp.int32, (SUBLANES, LANES), 0)
    low = {d: (sub & d) == 0 for d in (4, 2, 1)}

    def transpose8(a):
        for d in (4, 2, 1):
            nxt = list(a)
            for i in range(SUBLANES):
                if not i & d:
                    x, y = a[i], a[i + d]
                    nxt[i] = jnp.where(low[d], x, pltpu.roll(y, d, 0))
                    nxt[i + d] = jnp.where(low[d], pltpu.roll(x, SUBLANES - d, 0), y)
            a = nxt
        return a

    rows_per_iter = 2 * SUBLANES

    def body(it, carry):
        n0 = pl.multiple_of(it * rows_per_iter, rows_per_iter)
        for jb in range(a_ref.shape[1] // SUBLANES):
            js = pl.ds(jb * SUBLANES, SUBLANES)
            halves = [transpose8([a_ref[n0 + half * SUBLANES + i, js, :] for i in range(SUBLANES)])
                      for half in range(2)]
            for s in range(SUBLANES):
                kb, l = divmod(jb * SUBLANES + s, n_layers)
                tile = jnp.concatenate([halves[0][s], halves[1][s]], axis=0)
                o_ref[l, pl.ds(n0, rows_per_iter), kb * LANES:(kb + 1) * LANES] = tile.astype(o_ref.dtype)
        return carry

    lax.fori_loop(0, tn // rows_per_iter, body, 0)


def _native_rows(w_in):
    L, D, N = w_in.shape
    v = jnp.transpose(w_in.reshape(L, D // LANES, LANES, N), (3, 1, 0, 2))
    return v.reshape(N, (D // LANES) * L, LANES)


def _prep_rows(w_rows, L, D, tn, n, offset):
    return pl.pallas_call(
        _wprep_kernel,
        grid=(n,),
        in_specs=[pl.BlockSpec((pl.Element(tn), pl.Element(w_rows.shape[1]), pl.Element(LANES)),
                               lambda j: (offset(j), 0, 0))],
        out_specs=pl.BlockSpec((L, tn, D), lambda j: (0, j, 0)),
        out_shape=jax.ShapeDtypeStruct((L, n * tn, D), BF16),
        compiler_params=_cparams(("parallel",), 40),
        name="w_in_prep",
    )(w_rows)


def _prep_w_in(w_in):
    L, D, _ = w_in.shape
    f0 = N_ALIGNED * SLOT
    d0 = f0 + FA_HEADS
    g0 = d0 + SLOT

    def slot_row(j):
        return jnp.where(j < N_GATE_SLOTS, g0 + SLOT * j, jnp.where(j == J_DH, d0, SLOT * (j - N_GATE_SLOTS)))

    w_rows = _native_rows(w_in)
    return (_prep_rows(w_rows, L, D, SLOT, N_SLOTS, slot_row),
            _prep_rows(w_rows, L, D, LANES, 1, lambda j: f0 + 0 * j))


def _dot_nt(a, b):
    return lax.dot_general(a, b, (((1,), (1,)), ((), ())), preferred_element_type=F32)


def _inproj_kernel(*refs, n_cast):
    h_ref, g_ref, w_ref, wf_ref = refs[:4]
    cast_in = refs[4:4 + n_cast]
    out_ref, f_ref = refs[4 + n_cast:6 + n_cast]
    cast_out = refs[6 + n_cast:6 + 2 * n_cast]
    xn_ref = refs[-1]

    @pl.when(pl.program_id(1) == 0)
    def _():
        xn_ref[...] = _rms(h_ref[...], g_ref[...]).astype(BF16)
        f_ref[...] = _dot_nt(xn_ref[...], wf_ref[...])

    r = _dot_nt(xn_ref[...], w_ref[...])
    for c in range(out_ref.shape[0]):
        out_ref[c] = r[:, c * LANES:(c + 1) * LANES].astype(out_ref.dtype)
    for src, dst in zip(cast_in, cast_out):
        dst[...] = src[...].astype(dst.dtype)


def _inproj(h, g, w_t, wf_t, l, to_cast, tm=1024, tn=1280):
    T, D = h.shape
    N = w_t.shape[1]
    ni, nj = T // tm, N // tn
    n_blk = CAST_BLOCKS
    assert ni * nj >= n_blk
    row = lambda i, j: (i, 0)
    blk = lambda i, j: jnp.minimum(i * nj + j, n_blk - 1)
    cast_in, cast_out, cast_shape = [], [], []
    for w in to_cast:
        _, R, C = w.shape
        cast_in.append(pl.BlockSpec((None, R // n_blk, C), lambda i, j: (l, blk(i, j), 0)))
        cast_out.append(pl.BlockSpec((R // n_blk, C), lambda i, j: (blk(i, j), 0)))
        cast_shape.append(jax.ShapeDtypeStruct((R, C), BF16))
    res = pl.pallas_call(
        functools.partial(_inproj_kernel, n_cast=len(to_cast)),
        grid=(ni, nj),
        in_specs=[pl.BlockSpec((tm, D), row),
                  pl.BlockSpec((1, D), lambda i, j: (0, 0)),
                  pl.BlockSpec((None, tn, D), lambda i, j: (l, j, 0)),
                  pl.BlockSpec((None, LANES, D), lambda i, j: (l, 0, 0))] + cast_in,
        out_specs=[pl.BlockSpec((tn // LANES, tm, LANES), lambda i, j: (j, i, 0)),
                   pl.BlockSpec((tm, LANES), row)] + cast_out,
        out_shape=[jax.ShapeDtypeStruct((N // LANES, T, LANES), BF16),
                   jax.ShapeDtypeStruct((T, LANES), F32)] + cast_shape,
        scratch_shapes=[pltpu.VMEM((tm, D), BF16)],
        compiler_params=_cparams(("arbitrary", "arbitrary"), 56),
        name="inproj",
    )(h, g, w_t, wf_t, *to_cast)
    return res[0], res[1], res[2:]


def _gmlp_kernel(u_ref, v_ref, lng_ref, lnb_ref, ws_ref, bst_ref, o_ref):
    v = jnp.concatenate([v_ref[g] for g in range(GM_GROUPS)], axis=1).astype(F32)
    mu = jnp.mean(v, axis=-1, keepdims=True)
    vc = v - mu
    var = jnp.mean(vc * vc, axis=-1, keepdims=True)
    vn = (vc * lax.rsqrt(var + NORM_EPS) * lng_ref[...] + lnb_ref[...]).astype(BF16)
    r = lax.broadcasted_iota(jnp.int32, (GM_CHUNK, GM_CHUNK), 0)
    c = lax.broadcasted_iota(jnp.int32, (GM_CHUNK, GM_CHUNK), 1)
    causal = r >= c
    for g in range(GM_GROUPS):
        wg = jnp.where(causal, ws_ref[g], 0.0).astype(BF16)
        bcol = bst_ref[:, g:g + 1]
        cols = slice(g * HEAD_COLS, (g + 1) * HEAD_COLS)
        for n in range(v.shape[0] // GM_CHUNK):
            rows = slice(n * GM_CHUNK, (n + 1) * GM_CHUNK)
            mixed = jnp.dot(wg, vn[rows, cols], preferred_element_type=F32) + bcol
            o_ref[g, rows, :] = (u_ref[g, rows, :].astype(F32) * mixed).astype(o_ref.dtype)


def _gmlp(proj, ln_g, ln_b, w_s, b_s_t, tm=512):
    T = proj.shape[1]
    const2 = lambda i: (0, 0)
    grp = (SLOT_GROUPS, tm, HEAD_COLS)
    return pl.pallas_call(
        _gmlp_kernel,
        grid=(T // tm,),
        in_specs=[pl.BlockSpec(grp, lambda i: (J_U, i, 0)),
                  pl.BlockSpec(grp, lambda i: (J_V, i, 0)),
                  pl.BlockSpec((1, SLOT), const2),
                  pl.BlockSpec((1, SLOT), const2),
                  pl.BlockSpec((GM_GROUPS, GM_CHUNK, GM_CHUNK), lambda i: (0, 0, 0)),
                  pl.BlockSpec((GM_CHUNK, GM_GROUPS), const2)],
        out_specs=pl.BlockSpec(grp, lambda i: (0, i, 0)),
        out_shape=jax.ShapeDtypeStruct((SLOT_GROUPS, T, HEAD_COLS), BF16),
        compiler_params=_cparams(("parallel",)),
        name="gmlp",
    )(proj, proj, ln_g, ln_b, w_s, b_s_t)


def _qk(q, k):
    return lax.dot_general(q, k, (((1,), (1,)), ((), ())), preferred_element_type=F32)


def _causal(rows, tq):
    r = lax.broadcasted_iota(jnp.int32, (rows, tq), 0) % tq
    c = lax.broadcasted_iota(jnp.int32, (rows, tq), 1)
    return r >= c


def _long_short_order(n):
    lo, hi, out = 0, n - 1, []
    while lo <= hi:
        out.append(hi)
        hi -= 1
        if lo <= hi:
            out.append(lo)
            lo += 1
    return out


def _pipelined_blocks(order, logits, weights, finish):
    assert len(order) >= 2
    s_q = [logits(i) for i in order[:2]]
    w_q = [weights(order[0], s_q.pop(0))]
    for n, i in enumerate(order):
        if n + 2 < len(order):
            s_q.append(logits(order[n + 2]))
        if n + 1 < len(order):
            w_q.append(weights(order[n + 1], s_q.pop(0)))
        finish(i, *w_q.pop(0))


def _mask_last_block(s, causal):
    tq = causal.shape[1]
    diag = jnp.where(causal, s[:, -tq:], MASK_VALUE)
    return diag if s.shape[1] == tq else jnp.concatenate([s[:, :-tq], diag], axis=1)


def _row_softmax(s, row_const=None):
    m = jnp.max(s, axis=-1, keepdims=True)
    if row_const is not None:
        m = (m + row_const) - row_const
    p = jnp.exp2(s - m)
    return p, jnp.sum(p, axis=-1, keepdims=True)


def _diff_attn_kernel(q_ref, k_ref, v_ref, c_ref, sa_ref, sb_ref, lam_ref, g_ref, o_ref,
                      qs_ref, kr_ref, *, tq, lambda_init):
    S = q_ref.shape[0]
    c, sa, sb = c_ref[...], sa_ref[...], sb_ref[...]
    qf = _rotate(q_ref[...].astype(F32), c, sa, sb) * (DA_QK_DIM ** -0.5 * LOG2E)
    lane = lax.broadcasted_iota(jnp.int32, qf.shape, 1)
    qs_ref[0] = jnp.where(lane < DA_QK_DIM, qf, 0.0).astype(BF16)
    qs_ref[1] = jnp.where(lane >= DA_QK_DIM, qf, 0.0).astype(BF16)
    kr_ref[...] = _rotate(k_ref[...].astype(F32), c, sa, sb).astype(BF16)

    lp = lam_ref[...]

    def total(x):
        return jnp.sum(jnp.sum(x, axis=1, keepdims=True), axis=0, keepdims=True)

    lam = jnp.exp(total(lp[0:1] * lp[1:2])) - jnp.exp(total(lp[2:3] * lp[3:4])) + lambda_init
    causal = _causal(2 * tq, tq)
    g = g_ref[...]

    def logits(i):
        rows = pl.ds(i * tq, tq)
        q2 = jnp.concatenate([qs_ref[0, rows, :], qs_ref[1, rows, :]], axis=0)
        return _mask_last_block(_qk(q2, kr_ref[pl.ds(0, (i + 1) * tq), :]), causal)

    def weights(i, s):
        p, l = _row_softmax(s)
        ratio = lam * l[:tq] / l[tq:]
        return (p[:tq] - ratio * p[tq:]).astype(BF16), l[:tq]

    def finish(i, w, l1):
        o = jnp.dot(w, v_ref[pl.ds(0, (i + 1) * tq), :], preferred_element_type=F32) / l1
        o_ref[pl.ds(i * tq, tq), :] = (_rms(o, g) * (1.0 - lambda_init)).astype(o_ref.dtype)

    _pipelined_blocks(_long_short_order(S // tq), logits, weights, finish)


def _diff_attn(proj, tables, lam, subln_g, B, S, lambda_init, tq=256):
    T = proj.shape[1]
    per_slot = SLOT_GROUPS
    seq = lambda b, h: (b, 0)
    head = (None, S, HEAD_COLS)
    kern = functools.partial(_diff_attn_kernel, tq=tq, lambda_init=lambda_init)
    return pl.pallas_call(
        kern,
        grid=(B, DA_HEADS),
        in_specs=[pl.BlockSpec(head, lambda b, h: (J_BQ * per_slot + h, b, 0)),
                  pl.BlockSpec(head, lambda b, h: (J_BK * per_slot + h, b, 0)),
                  pl.BlockSpec(head, lambda b, h: (J_BV * per_slot + h, b, 0)),
                  pl.BlockSpec((S, LANES), seq),
                  pl.BlockSpec((S, LANES), seq),
                  pl.BlockSpec((S, LANES), seq),
                  pl.BlockSpec(lam.shape, lambda b, h: (0, 0)),
                  pl.BlockSpec((1, HEAD_COLS), lambda b, h: (0, 0))],
        out_specs=pl.BlockSpec(head, lambda b, h: (h, b, 0)),
        out_shape=jax.ShapeDtypeStruct((DA_HEADS, T, HEAD_COLS), BF16),
        scratch_shapes=[pltpu.VMEM((2, S, HEAD_COLS), BF16), pltpu.VMEM((S, HEAD_COLS), BF16)],
        compiler_params=_cparams(("parallel", "parallel")),
        name="diff_attn",
    )(proj, proj, proj, *tables, lam, subln_g)


def _fcum_kernel(f_ref, bf_ref, cum_ref, cumt_ref, *, blk):
    x = f_ref[...] + bf_ref[...]
    logf = jnp.minimum(x, 0.0) - jnp.log1p(jnp.exp(-jnp.abs(x)))
    r = lax.broadcasted_iota(jnp.int32, (blk, blk), 0)
    c = lax.broadcasted_iota(jnp.int32, (blk, blk), 1)
    tri = (r >= c).astype(F32)
    carry = jnp.zeros((1, LANES), F32)
    for n in range(x.shape[0] // blk):
        part = jnp.dot(tri, logf[n * blk:(n + 1) * blk], preferred_element_type=F32,
                       precision=lax.Precision.HIGHEST) + carry
        cum_ref[n * blk:(n + 1) * blk, :] = part
        carry = part[blk - 1:blk, :]
    cum_t = cum_ref[...].T
    for hh in range(FA_HEADS):
        cumt_ref[hh] = cum_t[hh:hh + 1, :]


def _fcum(f_logit, b_f, B, S):
    T = f_logit.shape[0]
    return pl.pallas_call(
        functools.partial(_fcum_kernel, blk=256),
        grid=(B,),
        in_specs=[pl.BlockSpec((S, LANES), lambda b: (b, 0)),
                  pl.BlockSpec((1, LANES), lambda b: (0, 0))],
        out_specs=[pl.BlockSpec((S, LANES), lambda b: (b, 0)),
                   pl.BlockSpec((FA_HEADS, 1, S), lambda b: (b, 0, 0))],
        out_shape=[jax.ShapeDtypeStruct((T, LANES), F32),
                   jax.ShapeDtypeStruct((B * FA_HEADS, 1, S), F32)],
        compiler_params=_cparams(("parallel",)),
        name="forget_cumsum",
    )(f_logit, b_f)


def _fox_attn_kernel(q_ref, k_ref, v_ref, cum_ref, cumt_ref, o_ref, qs_ref, *, tq):
    S = q_ref.shape[0]
    h = pl.program_id(1)
    qs_ref[...] = (q_ref[...].astype(F32) * (FA_HEAD_DIM ** -0.5 * LOG2E)).astype(BF16)
    causal = _causal(tq, tq)

    def logits(i):
        keys = pl.ds(0, (i + 1) * tq)
        s = _qk(qs_ref[pl.ds(i * tq, tq), :], k_ref[keys, :]) - cumt_ref[0, :, keys] * LOG2E
        return _mask_last_block(s, causal)

    def weights(i, s):
        cum = cum_ref[pl.ds(i * tq, tq), :]
        lane = lax.broadcasted_iota(jnp.int32, cum.shape, 1)
        cq = jnp.sum(jnp.where(lane == h, cum, 0.0), axis=-1, keepdims=True) * LOG2E
        p, l = _row_softmax(s, cq)
        return p.astype(BF16), l

    def finish(i, p, l):
        o = jnp.dot(p, v_ref[pl.ds(0, (i + 1) * tq), :], preferred_element_type=F32)
        o_ref[pl.ds(i * tq, tq), :] = (o / l).astype(o_ref.dtype)

    _pipelined_blocks(_long_short_order(S // tq), logits, weights, finish)


def _fox_attn(proj, cum, cum_t, B, S, tq=256):
    T = proj.shape[1]
    per_slot = SLOT_GROUPS
    head = (None, S, HEAD_COLS)
    return pl.pallas_call(
        functools.partial(_fox_attn_kernel, tq=tq),
        grid=(B, FA_HEADS),
        in_specs=[pl.BlockSpec(head, lambda b, h: (J_CQ * per_slot + h, b, 0)),
                  pl.BlockSpec(head, lambda b, h: (J_CK * per_slot + h, b, 0)),
                  pl.BlockSpec(head, lambda b, h: (J_CV * per_slot + h, b, 0)),
                  pl.BlockSpec((S, LANES), lambda b, h: (b, 0)),
                  pl.BlockSpec((1, 1, S), lambda b, h: (b * FA_HEADS + h, 0, 0))],
        out_specs=pl.BlockSpec(head, lambda b, h: (h, b, 0)),
        out_shape=jax.ShapeDtypeStruct((FA_HEADS, T, HEAD_COLS), BF16),
        scratch_shapes=[pltpu.VMEM((S, HEAD_COLS), BF16)],
        compiler_params=_cparams(("parallel", "parallel")),
        name="fox_attn",
    )(proj, proj, proj, cum, cum_t)


def _pool_kernel(h_ref, w_ref, sc_ref, o_ref):
    S = h_ref.shape[1]
    t = lax.broadcasted_iota(jnp.int32, (S, HEAD_COLS), 0)
    for g, win in enumerate(POOL_WINDOWS):
        cols = slice(g * HEAD_COLS, (g + 1) * HEAD_COLS)
        x = h_ref[g].astype(F32)
        tot = x
        span = 1
        while span < win:
            tot = tot + jnp.where(t >= span, pltpu.roll(tot, span, 0), 0.0)
            span *= 2
        cnt = jnp.minimum(t + 1, win).astype(F32)
        pooled = (tot / cnt - x).astype(BF16)
        y = jnp.dot(pooled, w_ref[g].astype(BF16), preferred_element_type=F32)
        o_ref[g] = (y * sc_ref[:, cols]).astype(o_ref.dtype)


def _pool(proj, w_pool, scale, B, S):
    T = proj.shape[1]
    grp = (SLOT_GROUPS, S, HEAD_COLS)
    return pl.pallas_call(
        _pool_kernel,
        grid=(B,),
        in_specs=[pl.BlockSpec(grp, lambda b: (J_DH, b, 0)),
                  pl.BlockSpec(w_pool.shape, lambda b: (0, 0, 0)),
                  pl.BlockSpec((1, SLOT), lambda b: (0, 0))],
        out_specs=pl.BlockSpec(grp, lambda b: (0, b, 0)),
        out_shape=jax.ShapeDtypeStruct((SLOT_GROUPS, T, HEAD_COLS), BF16),
        compiler_params=_cparams(("parallel",)),
        name="pool",
    )(proj, w_pool, scale)


def _merge_kernel(oa_ref, ob_ref, oc_ref, od_ref, gate_ref, wb_ref, wo_ref, g_ref, h_ref, out_ref):
    def wide(ref, g0, n):
        return jnp.concatenate([ref[g] for g in range(g0, g0 + n)], axis=1)

    per_branch = h_ref.shape[1] // HEAD_COLS
    merged = None
    for n, br in enumerate((oa_ref, ob_ref, oc_ref, od_ref)):
        bd = jnp.dot(wide(br, 0, SLOT_GROUPS), wb_ref[n], preferred_element_type=F32)
        term = jax.nn.sigmoid(wide(gate_ref, n * per_branch, per_branch).astype(F32)) * bd
        merged = term if merged is None else merged + term
    y = jnp.dot(merged.astype(BF16), wo_ref[...], preferred_element_type=F32)
    out_ref[...] = h_ref[...] + _rms(y, g_ref[...])


def _merge(branches, gates, w_branch, w_out, g, h, tm=256):
    T, D = h.shape
    row = lambda i: (i, 0)
    once = pl.Buffered(1)
    return pl.pallas_call(
        _merge_kernel,
        grid=(T // tm,),
        in_specs=[pl.BlockSpec((SLOT_GROUPS, tm, HEAD_COLS), lambda i: (0, i, 0))] * N_BRANCH + [
            pl.BlockSpec((N_BRANCH * D // HEAD_COLS, tm, HEAD_COLS), lambda i: (0, i, 0)),
            pl.BlockSpec(w_branch.shape, lambda i: (0, 0, 0), pipeline_mode=once),
            pl.BlockSpec(w_out.shape, lambda i: (0, 0), pipeline_mode=once),
            pl.BlockSpec((1, D), lambda i: (0, 0)),
            pl.BlockSpec((tm, D), row)],
        out_specs=pl.BlockSpec((tm, D), row),
        out_shape=jax.ShapeDtypeStruct((T, D), F32),
        compiler_params=_cparams(("parallel",), 48),
        name="merge_out",
    )(*branches, gates, w_branch, w_out, g, h)


def _ffn_kernel(h_ref, g1_ref, wu_ref, wd_ref, g2_ref, out_ref, hn_ref, acc_ref):
    f = pl.program_id(1)

    @pl.when(f == 0)
    def _():
        hn_ref[...] = _rms(h_ref[...], g1_ref[...]).astype(BF16)
        acc_ref[...] = jnp.zeros_like(acc_ref)

    up = jnp.dot(hn_ref[...], wu_ref[...], preferred_element_type=F32)
    a = jnp.square(jnp.maximum(up, 0.0)).astype(BF16)
    acc_ref[...] += jnp.dot(a, wd_ref[...], preferred_element_type=F32)

    @pl.when(f == pl.num_programs(1) - 1)
    def _():
        g2 = g2_ref[...]
        for r in _row_chunks(h_ref.shape[0]):
            out_ref[r, :] = h_ref[r, :] + _rms(acc_ref[r, :], g2)


def _ffn(h, g1, w_up, w_down, g2, tm=512, tf=1024):
    T, D = h.shape
    F = w_up.shape[1]
    row = lambda i, f: (i, 0)
    vec = lambda i, f: (0, 0)
    return pl.pallas_call(
        _ffn_kernel,
        grid=(T // tm, F // tf),
        in_specs=[pl.BlockSpec((tm, D), row),
                  pl.BlockSpec((1, D), vec),
                  pl.BlockSpec((D, tf), lambda i, f: (0, f)),
                  pl.BlockSpec((tf, D), lambda i, f: (f, 0)),
                  pl.BlockSpec((1, D), vec)],
        out_specs=pl.BlockSpec((tm, D), row),
        out_shape=jax.ShapeDtypeStruct((T, D), F32),
        scratch_shapes=[pltpu.VMEM((tm, D), BF16), pltpu.VMEM((tm, D), F32)],
        compiler_params=_cparams(("parallel", "arbitrary"), 48),
        name="ffn",
    )(h, g1, w_up, w_down, g2)


def kernel(x, positions, norm_mix_pre, norm_mix_post, norm_ffn_pre, norm_ffn_post, w_in, gm_ln_g,
           gm_ln_b, gm_w_s, gm_b_s, da_lambda, da_subln_g, fa_b_f, pool_w, pool_scale, w_branch,
           w_out, w_ffn_up, w_ffn_down):
    B, S, D = x.shape
    T = B * S
    h = x.reshape(T, D)
    tables = _rope_tables(positions)
    w_t, wf_t = _prep_w_in(w_in)
    L, NB, W, _ = w_branch.shape
    to_cast = (w_branch.reshape(L, NB * W, D), w_out, w_ffn_up, w_ffn_down)
    row = lambda a: a.reshape(1, -1)
    for l in range(DEPTH):
        lambda_init = 0.8 - 0.6 * math.exp(-0.3 * l)
        proj, f_logit, (wb, wo, wu, wd) = _inproj(h, row(norm_mix_pre[l]), w_t, wf_t, l, to_cast)

        o_a = _gmlp(proj, row(gm_ln_g[l]), row(gm_ln_b[l]), gm_w_s[l], gm_b_s[l].T)
        o_b = _diff_attn(proj, tables, da_lambda[l], row(da_subln_g[l]), B, S, lambda_init)
        b_f = jnp.pad(fa_b_f[l], (0, LANES - FA_HEADS)).reshape(1, LANES)
        cum, cum_t = _fcum(f_logit, b_f, B, S)
        o_c = _fox_attn(proj, cum, cum_t, B, S)
        o_d = _pool(proj, pool_w[l], row(pool_scale[l]), B, S)

        h = _merge((o_a, o_b, o_c, o_d), proj, wb.reshape(NB, W, D), wo, row(norm_mix_post[l]), h)
        h = _ffn(h, row(norm_ffn_pre[l]), wu, wd, row(norm_ffn_post[l]))
    return h.reshape(B, S, D)
```

```python
import functools
import math

import jax
import jax.numpy as jnp
from jax import lax
from jax.experimental import pallas as pl
from jax.experimental.pallas import tpu as pltpu

F32 = jnp.float32
BF16 = jnp.bfloat16

DEPTH = 2
N_BRANCH = 4
GM_GROUPS = 4
GM_CHUNK = 128
DA_HEADS = 4
DA_QK_DIM = 64
DA_ROT_DIM = 16
FA_HEADS = 4
FA_HEAD_DIM = 128
POOL_WINDOWS = (2, 4, 8, 16)
ROPE_THETA = 500000.0
NORM_EPS = 1e-6
MASK_VALUE = -1e30
LOG2E = math.log2(math.e)

LANES = 128
SUBLANES = 8
ROW_CHUNK = 2 * SUBLANES
CAST_BLOCKS = 64
HEAD_COLS = 128
SLOT = 512
SLOT_GROUPS = SLOT // HEAD_COLS
ATTN_HEADS_PER_STEP = 2


N_GATE_SLOTS = 16
J_U, J_V, J_BQ, J_BK, J_BV, J_CQ, J_CK, J_CV, J_DH = range(N_GATE_SLOTS, N_GATE_SLOTS + 9)
N_SLOTS = J_DH + 1
N_ALIGNED = 8


def _rms(x, g):
    return x * lax.rsqrt(jnp.mean(x * x, axis=-1, keepdims=True) + NORM_EPS) * g


def _row_chunks(n_rows, rows=ROW_CHUNK):
    return [pl.ds(r0, rows) for r0 in range(0, n_rows, rows)]


def _cparams(sem, vmem_mb=None):
    kw = dict(dimension_semantics=sem)
    if vmem_mb is not None:
        kw["vmem_limit_bytes"] = vmem_mb * 1024 * 1024
    return pltpu.CompilerParams(**kw)


def _rope_kernel(pos_ref, inv_ref, c_ref, sa_ref, sb_ref):
    ang = pos_ref[...].astype(F32) * inv_ref[...]
    sub = lax.broadcasted_iota(jnp.int32, ang.shape, 1) % DA_QK_DIM
    half = DA_ROT_DIM // 2
    s = jnp.sin(ang)
    c_ref[...] = jnp.cos(ang)
    sa_ref[...] = jnp.where(sub < half, -s, 0.0)
    sb_ref[...] = jnp.where((sub >= half) & (sub < 2 * half), s, 0.0)


def _rope_tables(positions):
    T = positions.size
    tm = 1024
    half = DA_ROT_DIM // 2
    inv = 1.0 / (ROPE_THETA ** (jnp.arange(0, DA_ROT_DIM, 2, dtype=F32) / DA_ROT_DIM))
    sub = jnp.arange(LANES) % DA_QK_DIM
    inv_lane = jnp.where(sub < 2 * half, inv[sub % half], 0.0).reshape(1, LANES)
    tab = jax.ShapeDtypeStruct((T, LANES), F32)
    return pl.pallas_call(
        _rope_kernel,
        grid=(T // tm,),
        in_specs=[pl.BlockSpec((tm, 1), lambda i: (i, 0)),
                  pl.BlockSpec((1, LANES), lambda i: (0, 0))],
        out_specs=[pl.BlockSpec((tm, LANES), lambda i: (i, 0))] * 3,
        out_shape=[tab] * 3,
        compiler_params=_cparams(("parallel",)),
        name="rope_tables",
    )(positions.reshape(T, 1), inv_lane)


def _rotate(x, c, sa, sb):
    half = DA_ROT_DIM // 2
    return x * c + pltpu.roll(x, LANES - half, 1) * sa + pltpu.roll(x, half, 1) * sb


def _wprep_kernel(a_ref, o_ref):
    n_layers, tn, _ = o_ref.shape
    sub = lax.broadcasted_iota(jnp.int32, (SUBLANES, LANES), 0)
    low = {d: (sub & d) == 0 for d in (4, 2, 1)}

    def transpose8(a):
        for d in (4, 2, 1):
            nxt = list(a)
            for i in range(SUBLANES):
                if not i & d:
                    x, y = a[i], a[i + d]
                    nxt[i] = jnp.where(low[d], x, pltpu.roll(y, d, 0))
                    nxt[i + d] = jnp.where(low[d], pltpu.roll(x, SUBLANES - d, 0), y)
            a = nxt
        return a

    rows_per_iter = 2 * SUBLANES

    def body(it, carry):
        n0 = pl.multiple_of(it * rows_per_iter, rows_per_iter)
        for jb in range(a_ref.shape[1] // SUBLANES):
            js = pl.ds(jb * SUBLANES, SUBLANES)
            halves = [transpose8([a_ref[n0 + half * SUBLANES + i, js, :] for i in range(SUBLANES)])
                      for half in range(2)]
            for s in range(SUBLANES):
                kb, l = divmod(jb * SUBLANES + s, n_layers)
                tile = jnp.concatenate([halves[0][s], halves[1][s]], axis=0)
                o_ref[l, pl.ds(n0, rows_per_iter), kb * LANES:(kb + 1) * LANES] = tile.astype(o_ref.dtype)
        return carry

    lax.fori_loop(0, tn // rows_per_iter, body, 0)


def _native_rows(w_in):
    L, D, N = w_in.shape
    v = jnp.transpose(w_in.reshape(L, D // LANES, LANES, N), (3, 1, 0, 2))
    return v.reshape(N, (D // LANES) * L, LANES)


def _prep_rows(w_rows, L, D, tn, n, offset):
    return pl.pallas_call(
        _wprep_kernel,
        grid=(n,),
        in_specs=[pl.BlockSpec((pl.Element(tn), pl.Element(w_rows.shape[1]), pl.Element(LANES)),
                               lambda j: (offset(j), 0, 0))],
        out_specs=pl.BlockSpec((L, tn, D), lambda j: (0, j, 0)),
        out_shape=jax.ShapeDtypeStruct((L, n * tn, D), BF16),
        compiler_params=_cparams(("parallel",), 40),
        name="w_in_prep",
    )(w_rows)


def _prep_w_in(w_in):
    L, D, _ = w_in.shape
    f0 = N_ALIGNED * SLOT
    d0 = f0 + FA_HEADS
    g0 = d0 + SLOT

    def slot_row(j):
        return jnp.where(j < N_GATE_SLOTS, g0 + SLOT * j, jnp.where(j == J_DH, d0, SLOT * (j - N_GATE_SLOTS)))

    w_rows = _native_rows(w_in)
    return (_prep_rows(w_rows, L, D, SLOT, N_SLOTS, slot_row),
            _prep_rows(w_rows, L, D, LANES, 1, lambda j: f0 + 0 * j))


def _dot_nt(a, b):
    return lax.dot_general(a, b, (((1,), (1,)), ((), ())), preferred_element_type=F32)


def _inproj_kernel(*refs, n_cast):
    h_ref, g_ref, w_ref, wf_ref = refs[:4]
    cast_in = refs[4:4 + n_cast]
    out_ref, f_ref = refs[4 + n_cast:6 + n_cast]
    cast_out = refs[6 + n_cast:6 + 2 * n_cast]
    xn_ref = refs[-1]

    @pl.when(pl.program_id(1) == 0)
    def _():
        xn_ref[...] = _rms(h_ref[...], g_ref[...]).astype(BF16)
        f_ref[...] = _dot_nt(xn_ref[...], wf_ref[...])

    r = _dot_nt(xn_ref[...], w_ref[...])
    for c in range(out_ref.shape[0]):
        out_ref[c] = r[:, c * LANES:(c + 1) * LANES].astype(out_ref.dtype)
    for src, dst in zip(cast_in, cast_out):
        dst[...] = src[...].astype(dst.dtype)


def _inproj(h, g, w_t, wf_t, l, to_cast, tm=1024, tn=1280):
    T, D = h.shape
    N = w_t.shape[1]
    ni, nj = T // tm, N // tn
    n_blk = CAST_BLOCKS
    assert ni * nj >= n_blk
    row = lambda i, j: (i, 0)
    blk = lambda i, j: jnp.minimum(i * nj + j, n_blk - 1)
    cast_in, cast_out, cast_shape = [], [], []
    for w in to_cast:
        _, R, C = w.shape
        cast_in.append(pl.BlockSpec((None, R // n_blk, C), lambda i, j: (l, blk(i, j), 0)))
        cast_out.append(pl.BlockSpec((R // n_blk, C), lambda i, j: (blk(i, j), 0)))
        cast_shape.append(jax.ShapeDtypeStruct((R, C), BF16))
    res = pl.pallas_call(
        functools.partial(_inproj_kernel, n_cast=len(to_cast)),
        grid=(ni, nj),
        in_specs=[pl.BlockSpec((tm, D), row),
                  pl.BlockSpec((1, D), lambda i, j: (0, 0)),
                  pl.BlockSpec((None, tn, D), lambda i, j: (l, j, 0)),
                  pl.BlockSpec((None, LANES, D), lambda i, j: (l, 0, 0))] + cast_in,
        out_specs=[pl.BlockSpec((tn // LANES, tm, LANES), lambda i, j: (j, i, 0)),
                   pl.BlockSpec((tm, LANES), row)] + cast_out,
        out_shape=[jax.ShapeDtypeStruct((N // LANES, T, LANES), BF16),
                   jax.ShapeDtypeStruct((T, LANES), F32)] + cast_shape,
        scratch_shapes=[pltpu.VMEM((tm, D), BF16)],
        compiler_params=_cparams(("arbitrary", "arbitrary"), 56),
        name="inproj",
    )(h, g, w_t, wf_t, *to_cast)
    return res[0], res[1], res[2:]


def _gmlp_kernel(u_ref, v_ref, lng_ref, lnb_ref, ws_ref, bst_ref, o_ref):
    v = jnp.concatenate([v_ref[g] for g in range(GM_GROUPS)], axis=1).astype(F32)
    mu = jnp.mean(v, axis=-1, keepdims=True)
    vc = v - mu
    var = jnp.mean(vc * vc, axis=-1, keepdims=True)
    vn = (vc * lax.rsqrt(var + NORM_EPS) * lng_ref[...] + lnb_ref[...]).astype(BF16)
    r = lax.broadcasted_iota(jnp.int32, (GM_CHUNK, GM_CHUNK), 0)
    c = lax.broadcasted_iota(jnp.int32, (GM_CHUNK, GM_CHUNK), 1)
    causal = r >= c
    for g in range(GM_GROUPS):
        wg = jnp.where(causal, ws_ref[g], 0.0).astype(BF16)
        bcol = bst_ref[:, g:g + 1]
        cols = slice(g * HEAD_COLS, (g + 1) * HEAD_COLS)
        for n in range(v.shape[0] // GM_CHUNK):
            rows = slice(n * GM_CHUNK, (n + 1) * GM_CHUNK)
            mixed = jnp.dot(wg, vn[rows, cols], preferred_element_type=F32) + bcol
            o_ref[g, rows, :] = (u_ref[g, rows, :].astype(F32) * mixed).astype(o_ref.dtype)


def _gmlp(proj, ln_g, ln_b, w_s, b_s_t, tm=2048):
    T = proj.shape[1]
    const2 = lambda i: (0, 0)
    grp = (SLOT_GROUPS, tm, HEAD_COLS)
    return pl.pallas_call(
        _gmlp_kernel,
        grid=(T // tm,),
        in_specs=[pl.BlockSpec(grp, lambda i: (J_U, i, 0)),
                  pl.BlockSpec(grp, lambda i: (J_V, i, 0)),
                  pl.BlockSpec((1, SLOT), const2),
                  pl.BlockSpec((1, SLOT), const2),
                  pl.BlockSpec((GM_GROUPS, GM_CHUNK, GM_CHUNK), lambda i: (0, 0, 0)),
                  pl.BlockSpec((GM_CHUNK, GM_GROUPS), const2)],
        out_specs=pl.BlockSpec(grp, lambda i: (0, i, 0)),
        out_shape=jax.ShapeDtypeStruct((SLOT_GROUPS, T, HEAD_COLS), BF16),
        compiler_params=_cparams(("parallel",)),
        name="gmlp",
    )(proj, proj, ln_g, ln_b, w_s, b_s_t)


def _qk(q, k):
    return lax.dot_general(q, k, (((1,), (1,)), ((), ())), preferred_element_type=F32)


def _causal(rows, tq):
    r = lax.broadcasted_iota(jnp.int32, (rows, tq), 0) % tq
    c = lax.broadcasted_iota(jnp.int32, (rows, tq), 1)
    return r >= c


def _long_short_order(n):
    lo, hi, out = 0, n - 1, []
    while lo <= hi:
        out.append(hi)
        hi -= 1
        if lo <= hi:
            out.append(lo)
            lo += 1
    return out


def _pipelined_blocks(order, logits, weights, finish):
    assert len(order) >= 2
    s_q = [logits(i) for i in order[:2]]
    w_q = [weights(order[0], s_q.pop(0))]
    for n, i in enumerate(order):
        if n + 2 < len(order):
            s_q.append(logits(order[n + 2]))
        if n + 1 < len(order):
            w_q.append(weights(order[n + 1], s_q.pop(0)))
        finish(i, *w_q.pop(0))


def _mask_last_block(s, causal):
    tq = causal.shape[1]
    diag = jnp.where(causal, s[:, -tq:], MASK_VALUE)
    return diag if s.shape[1] == tq else jnp.concatenate([s[:, :-tq], diag], axis=1)


def _row_softmax(s, row_const=None):
    m = jnp.max(s, axis=-1, keepdims=True)
    if row_const is not None:
        m = (m + row_const) - row_const
    p = jnp.exp2(s - m)
    return p, jnp.sum(p, axis=-1, keepdims=True)


def _diff_attn_kernel(q_ref, k_ref, v_ref, c_ref, sa_ref, sb_ref, lam_ref, g_ref, o_ref,
                      qs_ref, kr_ref, *, tq, lambda_init):
    n_heads, S, _ = q_ref.shape
    c, sa, sb = c_ref[...], sa_ref[...], sb_ref[...]
    lp = lam_ref[...]

    def total(x):
        return jnp.sum(jnp.sum(x, axis=1, keepdims=True), axis=0, keepdims=True)

    lam = jnp.exp(total(lp[0:1] * lp[1:2])) - jnp.exp(total(lp[2:3] * lp[3:4])) + lambda_init
    causal = _causal(2 * tq, tq)
    g = g_ref[...]
    lane = lax.broadcasted_iota(jnp.int32, (S, HEAD_COLS), 1)

    for hh in range(n_heads):
        qf = _rotate(q_ref[hh].astype(F32), c, sa, sb) * (DA_QK_DIM ** -0.5 * LOG2E)
        qs_ref[hh, 0] = jnp.where(lane < DA_QK_DIM, qf, 0.0).astype(BF16)
        qs_ref[hh, 1] = jnp.where(lane >= DA_QK_DIM, qf, 0.0).astype(BF16)
        kr_ref[hh] = _rotate(k_ref[hh].astype(F32), c, sa, sb).astype(BF16)

        def logits(i, hh=hh):
            rows = pl.ds(i * tq, tq)
            q2 = jnp.concatenate([qs_ref[hh, 0, rows, :], qs_ref[hh, 1, rows, :]], axis=0)
            return _mask_last_block(_qk(q2, kr_ref[hh, pl.ds(0, (i + 1) * tq), :]), causal)

        def weights(i, s):
            p, l = _row_softmax(s)
            ratio = lam * l[:tq] / l[tq:]
            return (p[:tq] - ratio * p[tq:]).astype(BF16), l[:tq]

        def finish(i, w, l1, hh=hh):
            o = jnp.dot(w, v_ref[hh, pl.ds(0, (i + 1) * tq), :], preferred_element_type=F32) / l1
            o_ref[hh, pl.ds(i * tq, tq), :] = (_rms(o, g) * (1.0 - lambda_init)).astype(o_ref.dtype)

        _pipelined_blocks(_long_short_order(S // tq), logits, weights, finish)


def _diff_attn(proj, tables, lam, subln_g, B, S, lambda_init, tq=256):
    T = proj.shape[1]
    hps = ATTN_HEADS_PER_STEP
    seq = lambda b, h: (b, 0)
    heads = (hps, S, HEAD_COLS)
    first = lambda slot: slot * SLOT_GROUPS // hps
    kern = functools.partial(_diff_attn_kernel, tq=tq, lambda_init=lambda_init)
    return pl.pallas_call(
        kern,
        grid=(B, DA_HEADS // hps),
        in_specs=[pl.BlockSpec(heads, lambda b, h: (first(J_BQ) + h, b, 0)),
                  pl.BlockSpec(heads, lambda b, h: (first(J_BK) + h, b, 0)),
                  pl.BlockSpec(heads, lambda b, h: (first(J_BV) + h, b, 0)),
                  pl.BlockSpec((S, LANES), seq),
                  pl.BlockSpec((S, LANES), seq),
                  pl.BlockSpec((S, LANES), seq),
                  pl.BlockSpec(lam.shape, lambda b, h: (0, 0)),
                  pl.BlockSpec((1, HEAD_COLS), lambda b, h: (0, 0))],
        out_specs=pl.BlockSpec(heads, lambda b, h: (h, b, 0)),
        out_shape=jax.ShapeDtypeStruct((DA_HEADS, T, HEAD_COLS), BF16),
        scratch_shapes=[pltpu.VMEM((hps, 2, S, HEAD_COLS), BF16), pltpu.VMEM((hps, S, HEAD_COLS), BF16)],
        compiler_params=_cparams(("parallel", "parallel")),
        name="diff_attn",
    )(proj, proj, proj, *tables, lam, subln_g)


def _fcum_kernel(f_ref, bf_ref, cum_ref, cumt_ref, *, blk):
    x = f_ref[...] + bf_ref[...]
    logf = jnp.minimum(x, 0.0) - jnp.log1p(jnp.exp(-jnp.abs(x)))
    r = lax.broadcasted_iota(jnp.int32, (blk, blk), 0)
    c = lax.broadcasted_iota(jnp.int32, (blk, blk), 1)
    tri = (r >= c).astype(F32)
    carry = jnp.zeros((1, LANES), F32)
    for n in range(x.shape[0] // blk):
        part = jnp.dot(tri, logf[n * blk:(n + 1) * blk], preferred_element_type=F32,
                       precision=lax.Precision.HIGHEST) + carry
        cum_ref[n * blk:(n + 1) * blk, :] = part
        carry = part[blk - 1:blk, :]
    cum_t = cum_ref[...].T
    for hh in range(FA_HEADS):
        cumt_ref[hh] = cum_t[hh:hh + 1, :]


def _fcum(f_logit, b_f, B, S):
    T = f_logit.shape[0]
    return pl.pallas_call(
        functools.partial(_fcum_kernel, blk=256),
        grid=(B,),
        in_specs=[pl.BlockSpec((S, LANES), lambda b: (b, 0)),
                  pl.BlockSpec((1, LANES), lambda b: (0, 0))],
        out_specs=[pl.BlockSpec((S, LANES), lambda b: (b, 0)),
                   pl.BlockSpec((FA_HEADS, 1, S), lambda b: (b, 0, 0))],
        out_shape=[jax.ShapeDtypeStruct((T, LANES), F32),
                   jax.ShapeDtypeStruct((B * FA_HEADS, 1, S), F32)],
        compiler_params=_cparams(("parallel",)),
        name="forget_cumsum",
    )(f_logit, b_f)


def _fox_attn_kernel(q_ref, k_ref, v_ref, cum_ref, cumt_ref, o_ref, qs_ref, *, tq):
    n_heads, S, _ = q_ref.shape
    causal = _causal(tq, tq)
    for hh in range(n_heads):
        head = pl.program_id(1) * n_heads + hh
        qs_ref[hh] = (q_ref[hh].astype(F32) * (FA_HEAD_DIM ** -0.5 * LOG2E)).astype(BF16)

        def logits(i, hh=hh):
            keys = pl.ds(0, (i + 1) * tq)
            s = _qk(qs_ref[hh, pl.ds(i * tq, tq), :], k_ref[hh, keys, :]) - cumt_ref[hh, :, keys] * LOG2E
            return _mask_last_block(s, causal)

        def weights(i, s, head=head):
            cum = cum_ref[pl.ds(i * tq, tq), :]
            lane = lax.broadcasted_iota(jnp.int32, cum.shape, 1)
            cq = jnp.sum(jnp.where(lane == head, cum, 0.0), axis=-1, keepdims=True) * LOG2E
            p, l = _row_softmax(s, cq)
            return p.astype(BF16), l

        def finish(i, p, l, hh=hh):
            o = jnp.dot(p, v_ref[hh, pl.ds(0, (i + 1) * tq), :], preferred_element_type=F32)
            o_ref[hh, pl.ds(i * tq, tq), :] = (o / l).astype(o_ref.dtype)

        _pipelined_blocks(_long_short_order(S // tq), logits, weights, finish)


def _fox_attn(proj, cum, cum_t, B, S, tq=256):
    T = proj.shape[1]
    hps = ATTN_HEADS_PER_STEP
    heads = (hps, S, HEAD_COLS)
    first = lambda slot: slot * SLOT_GROUPS // hps
    return pl.pallas_call(
        functools.partial(_fox_attn_kernel, tq=tq),
        grid=(B, FA_HEADS // hps),
        in_specs=[pl.BlockSpec(heads, lambda b, h: (first(J_CQ) + h, b, 0)),
                  pl.BlockSpec(heads, lambda b, h: (first(J_CK) + h, b, 0)),
                  pl.BlockSpec(heads, lambda b, h: (first(J_CV) + h, b, 0)),
                  pl.BlockSpec((S, LANES), lambda b, h: (b, 0)),
                  pl.BlockSpec((hps, 1, S), lambda b, h: (b * (FA_HEADS // hps) + h, 0, 0))],
        out_specs=pl.BlockSpec(heads, lambda b, h: (h, b, 0)),
        out_shape=jax.ShapeDtypeStruct((FA_HEADS, T, HEAD_COLS), BF16),
        scratch_shapes=[pltpu.VMEM((hps, S, HEAD_COLS), BF16)],
        compiler_params=_cparams(("parallel", "parallel")),
        name="fox_attn",
    )(proj, proj, proj, cum, cum_t)


def _pool_kernel(h_ref, w_ref, sc_ref, o_ref):
    S = h_ref.shape[1]
    t = lax.broadcasted_iota(jnp.int32, (S, HEAD_COLS), 0)
    for g, win in enumerate(POOL_WINDOWS):
        cols = slice(g * HEAD_COLS, (g + 1) * HEAD_COLS)
        x = h_ref[g].astype(F32)
        tot = x
        span = 1
        while span < win:
            tot = tot + jnp.where(t >= span, pltpu.roll(tot, span, 0), 0.0)
            span *= 2
        cnt = jnp.minimum(t + 1, win).astype(F32)
        pooled = (tot / cnt - x).astype(BF16)
        y = jnp.dot(pooled, w_ref[g].astype(BF16), preferred_element_type=F32)
        o_ref[g] = (y * sc_ref[:, cols]).astype(o_ref.dtype)


def _pool(proj, w_pool, scale, B, S):
    T = proj.shape[1]
    grp = (SLOT_GROUPS, S, HEAD_COLS)
    return pl.pallas_call(
        _pool_kernel,
        grid=(B,),
        in_specs=[pl.BlockSpec(grp, lambda b: (J_DH, b, 0)),
                  pl.BlockSpec(w_pool.shape, lambda b: (0, 0, 0)),
                  pl.BlockSpec((1, SLOT), lambda b: (0, 0))],
        out_specs=pl.BlockSpec(grp, lambda b: (0, b, 0)),
        out_shape=jax.ShapeDtypeStruct((SLOT_GROUPS, T, HEAD_COLS), BF16),
        compiler_params=_cparams(("parallel",)),
        name="pool",
    )(proj, w_pool, scale)


def _merge_kernel(oa_ref, ob_ref, oc_ref, od_ref, gate_ref, wb_ref, wo_ref, g_ref, h_ref, out_ref):
    def wide(ref, g0, n):
        return jnp.concatenate([ref[g] for g in range(g0, g0 + n)], axis=1)

    per_branch = h_ref.shape[1] // HEAD_COLS
    merged = None
    for n, br in enumerate((oa_ref, ob_ref, oc_ref, od_ref)):
        bd = jnp.dot(wide(br, 0, SLOT_GROUPS), wb_ref[n], preferred_element_type=F32)
        term = jax.nn.sigmoid(wide(gate_ref, n * per_branch, per_branch).astype(F32)) * bd
        merged = term if merged is None else merged + term
    y = jnp.dot(merged.astype(BF16), wo_ref[...], preferred_element_type=F32)
    out_ref[...] = h_ref[...] + _rms(y, g_ref[...])


def _merge(branches, gates, w_branch, w_out, g, h, tm=256):
    T, D = h.shape
    row = lambda i: (i, 0)
    once = pl.Buffered(1)
    return pl.pallas_call(
        _merge_kernel,
        grid=(T // tm,),
        in_specs=[pl.BlockSpec((SLOT_GROUPS, tm, HEAD_COLS), lambda i: (0, i, 0))] * N_BRANCH + [
            pl.BlockSpec((N_BRANCH * D // HEAD_COLS, tm, HEAD_COLS), lambda i: (0, i, 0)),
            pl.BlockSpec(w_branch.shape, lambda i: (0, 0, 0), pipeline_mode=once),
            pl.BlockSpec(w_out.shape, lambda i: (0, 0), pipeline_mode=once),
            pl.BlockSpec((1, D), lambda i: (0, 0)),
            pl.BlockSpec((tm, D), row)],
        out_specs=pl.BlockSpec((tm, D), row),
        out_shape=jax.ShapeDtypeStruct((T, D), F32),
        compiler_params=_cparams(("parallel",), 48),
        name="merge_out",
    )(*branches, gates, w_branch, w_out, g, h)


def _ffn_kernel(h_ref, g1_ref, wu_ref, wd_ref, g2_ref, out_ref, hn_ref, acc_ref):
    f = pl.program_id(1)

    @pl.when(f == 0)
    def _():
        hn_ref[...] = _rms(h_ref[...], g1_ref[...]).astype(BF16)
        acc_ref[...] = jnp.zeros_like(acc_ref)

    up = jnp.dot(hn_ref[...], wu_ref[...], preferred_element_type=F32)
    a = jnp.square(jnp.maximum(up, 0.0)).astype(BF16)
    acc_ref[...] += jnp.dot(a, wd_ref[...], preferred_element_type=F32)

    @pl.when(f == pl.num_programs(1) - 1)
    def _():
        g2 = g2_ref[...]
        for r in _row_chunks(h_ref.shape[0]):
            out_ref[r, :] = h_ref[r, :] + _rms(acc_ref[r, :], g2)


def _ffn(h, g1, w_up, w_down, g2, tm=512, tf=1024):
    T, D = h.shape
    F = w_up.shape[1]
    row = lambda i, f: (i, 0)
    vec = lambda i, f: (0, 0)
    return pl.pallas_call(
        _ffn_kernel,
        grid=(T // tm, F // tf),
        in_specs=[pl.BlockSpec((tm, D), row),
                  pl.BlockSpec((1, D), vec),
                  pl.BlockSpec((D, tf), lambda i, f: (0, f)),
                  pl.BlockSpec((tf, D), lambda i, f: (f, 0)),
                  pl.BlockSpec((1, D), vec)],
        out_specs=pl.BlockSpec((tm, D), row),
        out_shape=jax.ShapeDtypeStruct((T, D), F32),
        scratch_shapes=[pltpu.VMEM((tm, D), BF16), pltpu.VMEM((tm, D), F32)],
        compiler_params=_cparams(("parallel", "arbitrary"), 48),
        name="ffn",
    )(h, g1, w_up, w_down, g2)


def kernel(x, positions, norm_mix_pre, norm_mix_post, norm_ffn_pre, norm_ffn_post, w_in, gm_ln_g,
           gm_ln_b, gm_w_s, gm_b_s, da_lambda, da_subln_g, fa_b_f, pool_w, pool_scale, w_branch,
           w_out, w_ffn_up, w_ffn_down):
    B, S, D = x.shape
    T = B * S
    h = x.reshape(T, D)
    tables = _rope_tables(positions)
    w_t, wf_t = _prep_w_in(w_in)
    L, NB, W, _ = w_branch.shape
    to_cast = (w_branch.reshape(L, NB * W, D), w_out, w_ffn_up, w_ffn_down)
    row = lambda a: a.reshape(1, -1)
    for l in range(DEPTH):
        lambda_init = 0.8 - 0.6 * math.exp(-0.3 * l)
        proj, f_logit, (wb, wo, wu, wd) = _inproj(h, row(norm_mix_pre[l]), w_t, wf_t, l, to_cast)

        o_a = _gmlp(proj, row(gm_ln_g[l]), row(gm_ln_b[l]), gm_w_s[l], gm_b_s[l].T)
        o_b = _diff_attn(proj, tables, da_lambda[l], row(da_subln_g[l]), B, S, lambda_init)
        b_f = jnp.pad(fa_b_f[l], (0, LANES - FA_HEADS)).reshape(1, LANES)
        cum, cum_t = _fcum(f_logit, b_f, B, S)
        o_c = _fox_attn(proj, cum, cum_t, B, S)
        o_d = _pool(proj, pool_w[l], row(pool_scale[l]), B, S)

        h = _merge((o_a, o_b, o_c, o_d), proj, wb.reshape(NB, W, D), wo, row(norm_mix_post[l]), h)
        h = _ffn(h, row(norm_ffn_pre[l]), wu, wd, row(norm_ffn_post[l]))
    return h.reshape(B, S, D)
```

```python
import functools
import math

import jax
import jax.numpy as jnp
from jax import lax
from jax.experimental import pallas as pl
from jax.experimental.pallas import tpu as pltpu

F32 = jnp.float32
BF16 = jnp.bfloat16

DEPTH = 2
N_BRANCH = 4
GM_GROUPS = 4
GM_CHUNK = 128
DA_HEADS = 4
DA_QK_DIM = 64
DA_ROT_DIM = 16
FA_HEADS = 4
FA_HEAD_DIM = 128
POOL_WINDOWS = (2, 4, 8, 16)
ROPE_THETA = 500000.0
NORM_EPS = 1e-6
MASK_VALUE = -1e30
LOG2E = math.log2(math.e)

LANES = 128
SUBLANES = 8
ROW_CHUNK = 2 * SUBLANES
CAST_BLOCKS = 64
HEAD_COLS = 128
SLOT = 512
SLOT_GROUPS = SLOT // HEAD_COLS
ATTN_HEADS_PER_STEP = 2

ROPE_TM = 1024
PREP_VMEM_MB = 40
INPROJ_TM, INPROJ_TN, INPROJ_VMEM_MB = 1024, 1280, 56
GMLP_TM = 2048
ATTN_TQ = 256
FCUM_BLOCK = 256
MERGE_TM, MERGE_VMEM_MB = 256, 48
FFN_TM, FFN_TF, FFN_VMEM_MB = 512, 1024, 48


N_GATE_SLOTS = 16
J_U, J_V, J_BQ, J_BK, J_BV, J_CQ, J_CK, J_CV, J_DH = range(N_GATE_SLOTS, N_GATE_SLOTS + 9)
N_SLOTS = J_DH + 1
N_ALIGNED = 8


def _rms(x, g):
    return x * lax.rsqrt(jnp.mean(x * x, axis=-1, keepdims=True) + NORM_EPS) * g


def _row_chunks(n_rows, rows=ROW_CHUNK):
    return [pl.ds(r0, rows) for r0 in range(0, n_rows, rows)]


def _cparams(sem, vmem_mb=None):
    kw = dict(dimension_semantics=sem)
    if vmem_mb is not None:
        kw["vmem_limit_bytes"] = vmem_mb * 1024 * 1024
    return pltpu.CompilerParams(**kw)


def _rope_kernel(pos_ref, inv_ref, c_ref, sa_ref, sb_ref):
    ang = pos_ref[...].astype(F32) * inv_ref[...]
    sub = lax.broadcasted_iota(jnp.int32, ang.shape, 1) % DA_QK_DIM
    half = DA_ROT_DIM // 2
    s = jnp.sin(ang)
    c_ref[...] = jnp.cos(ang)
    sa_ref[...] = jnp.where(sub < half, -s, 0.0)
    sb_ref[...] = jnp.where((sub >= half) & (sub < 2 * half), s, 0.0)


def _rope_tables(positions):
    T = positions.size
    tm = ROPE_TM
    half = DA_ROT_DIM // 2
    inv = 1.0 / (ROPE_THETA ** (jnp.arange(0, DA_ROT_DIM, 2, dtype=F32) / DA_ROT_DIM))
    sub = jnp.arange(LANES) % DA_QK_DIM
    inv_lane = jnp.where(sub < 2 * half, inv[sub % half], 0.0).reshape(1, LANES)
    tab = jax.ShapeDtypeStruct((T, LANES), F32)
    return pl.pallas_call(
        _rope_kernel,
        grid=(T // tm,),
        in_specs=[pl.BlockSpec((tm, 1), lambda i: (i, 0)),
                  pl.BlockSpec((1, LANES), lambda i: (0, 0))],
        out_specs=[pl.BlockSpec((tm, LANES), lambda i: (i, 0))] * 3,
        out_shape=[tab] * 3,
        compiler_params=_cparams(("parallel",)),
        name="rope_tables",
    )(positions.reshape(T, 1), inv_lane)


def _rotate(x, c, sa, sb):
    half = DA_ROT_DIM // 2
    return x * c + pltpu.roll(x, LANES - half, 1) * sa + pltpu.roll(x, half, 1) * sb


def _wprep_kernel(a_ref, o_ref):
    n_layers, tn, _ = o_ref.shape
    sub = lax.broadcasted_iota(jnp.int32, (SUBLANES, LANES), 0)
    low = {d: (sub & d) == 0 for d in (4, 2, 1)}

    def transpose8(a):
        for d in (4, 2, 1):
            nxt = list(a)
            for i in range(SUBLANES):
                if not i & d:
                    x, y = a[i], a[i + d]
                    nxt[i] = jnp.where(low[d], x, pltpu.roll(y, d, 0))
                    nxt[i + d] = jnp.where(low[d], pltpu.roll(x, SUBLANES - d, 0), y)
            a = nxt
        return a

    rows_per_iter = 2 * SUBLANES

    def body(it, carry):
        n0 = pl.multiple_of(it * rows_per_iter, rows_per_iter)
        for jb in range(a_ref.shape[1] // SUBLANES):
            js = pl.ds(jb * SUBLANES, SUBLANES)
            halves = [transpose8([a_ref[n0 + half * SUBLANES + i, js, :] for i in range(SUBLANES)])
                      for half in range(2)]
            for s in range(SUBLANES):
                kb, l = divmod(jb * SUBLANES + s, n_layers)
                tile = jnp.concatenate([halves[0][s], halves[1][s]], axis=0)
                o_ref[l, pl.ds(n0, rows_per_iter), kb * LANES:(kb + 1) * LANES] = tile.astype(o_ref.dtype)
        return carry

    lax.fori_loop(0, tn // rows_per_iter, body, 0)


def _native_rows(w_in):
    L, D, N = w_in.shape
    v = jnp.transpose(w_in.reshape(L, D // LANES, LANES, N), (3, 1, 0, 2))
    return v.reshape(N, (D // LANES) * L, LANES)


def _prep_rows(w_rows, L, D, tn, n, offset):
    return pl.pallas_call(
        _wprep_kernel,
        grid=(n,),
        in_specs=[pl.BlockSpec((pl.Element(tn), pl.Element(w_rows.shape[1]), pl.Element(LANES)),
                               lambda j: (offset(j), 0, 0))],
        out_specs=pl.BlockSpec((L, tn, D), lambda j: (0, j, 0)),
        out_shape=jax.ShapeDtypeStruct((L, n * tn, D), BF16),
        compiler_params=_cparams(("parallel",), PREP_VMEM_MB),
        name="w_in_prep",
    )(w_rows)


def _prep_w_in(w_in):
    L, D, _ = w_in.shape
    f0 = N_ALIGNED * SLOT
    d0 = f0 + FA_HEADS
    g0 = d0 + SLOT

    def slot_row(j):
        return jnp.where(j < N_GATE_SLOTS, g0 + SLOT * j, jnp.where(j == J_DH, d0, SLOT * (j - N_GATE_SLOTS)))

    w_rows = _native_rows(w_in)
    return (_prep_rows(w_rows, L, D, SLOT, N_SLOTS, slot_row),
            _prep_rows(w_rows, L, D, LANES, 1, lambda j: f0 + 0 * j))


def _dot_nt(a, b):
    return lax.dot_general(a, b, (((1,), (1,)), ((), ())), preferred_element_type=F32)


def _inproj_kernel(*refs, n_cast):
    h_ref, g_ref, w_ref, wf_ref = refs[:4]
    cast_in = refs[4:4 + n_cast]
    out_ref, f_ref = refs[4 + n_cast:6 + n_cast]
    cast_out = refs[6 + n_cast:6 + 2 * n_cast]
    xn_ref = refs[-1]

    @pl.when(pl.program_id(1) == 0)
    def _():
        xn_ref[...] = _rms(h_ref[...], g_ref[...]).astype(BF16)
        f_ref[...] = _dot_nt(xn_ref[...], wf_ref[...])

    r = _dot_nt(xn_ref[...], w_ref[...])
    for c in range(out_ref.shape[0]):
        out_ref[c] = r[:, c * LANES:(c + 1) * LANES].astype(out_ref.dtype)
    for src, dst in zip(cast_in, cast_out):
        dst[...] = src[...].astype(dst.dtype)


def _inproj(h, g, w_t, wf_t, l, to_cast, tm=INPROJ_TM, tn=INPROJ_TN):
    T, D = h.shape
    N = w_t.shape[1]
    ni, nj = T // tm, N // tn
    n_blk = CAST_BLOCKS
    assert ni * nj >= n_blk
    row = lambda i, j: (i, 0)
    blk = lambda i, j: jnp.minimum(i * nj + j, n_blk - 1)
    cast_in, cast_out, cast_shape = [], [], []
    for w in to_cast:
        _, R, C = w.shape
        cast_in.append(pl.BlockSpec((None, R // n_blk, C), lambda i, j: (l, blk(i, j), 0)))
        cast_out.append(pl.BlockSpec((R // n_blk, C), lambda i, j: (blk(i, j), 0)))
        cast_shape.append(jax.ShapeDtypeStruct((R, C), BF16))
    res = pl.pallas_call(
        functools.partial(_inproj_kernel, n_cast=len(to_cast)),
        grid=(ni, nj),
        in_specs=[pl.BlockSpec((tm, D), row),
                  pl.BlockSpec((1, D), lambda i, j: (0, 0)),
                  pl.BlockSpec((None, tn, D), lambda i, j: (l, j, 0)),
                  pl.BlockSpec((None, LANES, D), lambda i, j: (l, 0, 0))] + cast_in,
        out_specs=[pl.BlockSpec((tn // LANES, tm, LANES), lambda i, j: (j, i, 0)),
                   pl.BlockSpec((tm, LANES), row)] + cast_out,
        out_shape=[jax.ShapeDtypeStruct((N // LANES, T, LANES), BF16),
                   jax.ShapeDtypeStruct((T, LANES), F32)] + cast_shape,
        scratch_shapes=[pltpu.VMEM((tm, D), BF16)],
        compiler_params=_cparams(("arbitrary", "arbitrary"), INPROJ_VMEM_MB),
        name="inproj",
    )(h, g, w_t, wf_t, *to_cast)
    return res[0], res[1], res[2:]


def _gmlp_kernel(u_ref, v_ref, lng_ref, lnb_ref, ws_ref, bst_ref, o_ref):
    v = jnp.concatenate([v_ref[g] for g in range(GM_GROUPS)], axis=1).astype(F32)
    mu = jnp.mean(v, axis=-1, keepdims=True)
    vc = v - mu
    var = jnp.mean(vc * vc, axis=-1, keepdims=True)
    vn = (vc * lax.rsqrt(var + NORM_EPS) * lng_ref[...] + lnb_ref[...]).astype(BF16)
    r = lax.broadcasted_iota(jnp.int32, (GM_CHUNK, GM_CHUNK), 0)
    c = lax.broadcasted_iota(jnp.int32, (GM_CHUNK, GM_CHUNK), 1)
    causal = r >= c
    for g in range(GM_GROUPS):
        wg = jnp.where(causal, ws_ref[g], 0.0).astype(BF16)
        bcol = bst_ref[:, g:g + 1]
        cols = slice(g * HEAD_COLS, (g + 1) * HEAD_COLS)
        for n in range(v.shape[0] // GM_CHUNK):
            rows = slice(n * GM_CHUNK, (n + 1) * GM_CHUNK)
            mixed = jnp.dot(wg, vn[rows, cols], preferred_element_type=F32) + bcol
            o_ref[g, rows, :] = (u_ref[g, rows, :].astype(F32) * mixed).astype(o_ref.dtype)


def _gmlp(proj, ln_g, ln_b, w_s, b_s_t, tm=GMLP_TM):
    T = proj.shape[1]
    const2 = lambda i: (0, 0)
    grp = (SLOT_GROUPS, tm, HEAD_COLS)
    return pl.pallas_call(
        _gmlp_kernel,
        grid=(T // tm,),
        in_specs=[pl.BlockSpec(grp, lambda i: (J_U, i, 0)),
                  pl.BlockSpec(grp, lambda i: (J_V, i, 0)),
                  pl.BlockSpec((1, SLOT), const2),
                  pl.BlockSpec((1, SLOT), const2),
                  pl.BlockSpec((GM_GROUPS, GM_CHUNK, GM_CHUNK), lambda i: (0, 0, 0)),
                  pl.BlockSpec((GM_CHUNK, GM_GROUPS), const2)],
        out_specs=pl.BlockSpec(grp, lambda i: (0, i, 0)),
        out_shape=jax.ShapeDtypeStruct((SLOT_GROUPS, T, HEAD_COLS), BF16),
        compiler_params=_cparams(("parallel",)),
        name="gmlp",
    )(proj, proj, ln_g, ln_b, w_s, b_s_t)


def _qk(q, k):
    return lax.dot_general(q, k, (((1,), (1,)), ((), ())), preferred_element_type=F32)


def _causal(rows, tq):
    r = lax.broadcasted_iota(jnp.int32, (rows, tq), 0) % tq
    c = lax.broadcasted_iota(jnp.int32, (rows, tq), 1)
    return r >= c


def _long_short_order(n):
    lo, hi, out = 0, n - 1, []
    while lo <= hi:
        out.append(hi)
        hi -= 1
        if lo <= hi:
            out.append(lo)
            lo += 1
    return out


def _pipelined_blocks(order, logits, weights, finish):
    assert len(order) >= 2
    s_q = [logits(i) for i in order[:2]]
    w_q = [weights(order[0], s_q.pop(0))]
    for n, i in enumerate(order):
        if n + 2 < len(order):
            s_q.append(logits(order[n + 2]))
        if n + 1 < len(order):
            w_q.append(weights(order[n + 1], s_q.pop(0)))
        finish(i, *w_q.pop(0))


def _mask_last_block(s, causal):
    tq = causal.shape[1]
    diag = jnp.where(causal, s[:, -tq:], MASK_VALUE)
    return diag if s.shape[1] == tq else jnp.concatenate([s[:, :-tq], diag], axis=1)


def _row_softmax(s, row_const=None):
    m = jnp.max(s, axis=-1, keepdims=True)
    if row_const is not None:
        m = (m + row_const) - row_const
    p = jnp.exp2(s - m)
    return p, jnp.sum(p, axis=-1, keepdims=True)


def _diff_attn_kernel(q_ref, k_ref, v_ref, c_ref, sa_ref, sb_ref, lam_ref, g_ref, o_ref,
                      qs_ref, kr_ref, *, tq, lambda_init):
    n_heads, S, _ = q_ref.shape
    c, sa, sb = c_ref[...], sa_ref[...], sb_ref[...]
    lp = lam_ref[...]

    def total(x):
        return jnp.sum(jnp.sum(x, axis=1, keepdims=True), axis=0, keepdims=True)

    lam = jnp.exp(total(lp[0:1] * lp[1:2])) - jnp.exp(total(lp[2:3] * lp[3:4])) + lambda_init
    causal = _causal(2 * tq, tq)
    g = g_ref[...]
    lane = lax.broadcasted_iota(jnp.int32, (S, HEAD_COLS), 1)

    for hh in range(n_heads):
        qf = _rotate(q_ref[hh].astype(F32), c, sa, sb) * (DA_QK_DIM ** -0.5 * LOG2E)
        qs_ref[hh, 0] = jnp.where(lane < DA_QK_DIM, qf, 0.0).astype(BF16)
        qs_ref[hh, 1] = jnp.where(lane >= DA_QK_DIM, qf, 0.0).astype(BF16)
        kr_ref[hh] = _rotate(k_ref[hh].astype(F32), c, sa, sb).astype(BF16)

        def logits(i, hh=hh):
            rows = pl.ds(i * tq, tq)
            q2 = jnp.concatenate([qs_ref[hh, 0, rows, :], qs_ref[hh, 1, rows, :]], axis=0)
            return _mask_last_block(_qk(q2, kr_ref[hh, pl.ds(0, (i + 1) * tq), :]), causal)

        def weights(i, s):
            p, l = _row_softmax(s)
            ratio = lam * l[:tq] / l[tq:]
            return (p[:tq] - ratio * p[tq:]).astype(BF16), l[:tq]

        def finish(i, w, l1, hh=hh):
            o = jnp.dot(w, v_ref[hh, pl.ds(0, (i + 1) * tq), :], preferred_element_type=F32) / l1
            o_ref[hh, pl.ds(i * tq, tq), :] = (_rms(o, g) * (1.0 - lambda_init)).astype(o_ref.dtype)

        _pipelined_blocks(_long_short_order(S // tq), logits, weights, finish)


def _diff_attn(proj, tables, lam, subln_g, B, S, lambda_init, tq=ATTN_TQ):
    T = proj.shape[1]
    hps = ATTN_HEADS_PER_STEP
    seq = lambda b, h: (b, 0)
    heads = (hps, S, HEAD_COLS)
    first = lambda slot: slot * SLOT_GROUPS // hps
    kern = functools.partial(_diff_attn_kernel, tq=tq, lambda_init=lambda_init)
    return pl.pallas_call(
        kern,
        grid=(B, DA_HEADS // hps),
        in_specs=[pl.BlockSpec(heads, lambda b, h: (first(J_BQ) + h, b, 0)),
                  pl.BlockSpec(heads, lambda b, h: (first(J_BK) + h, b, 0)),
                  pl.BlockSpec(heads, lambda b, h: (first(J_BV) + h, b, 0)),
                  pl.BlockSpec((S, LANES), seq),
                  pl.BlockSpec((S, LANES), seq),
                  pl.BlockSpec((S, LANES), seq),
                  pl.BlockSpec(lam.shape, lambda b, h: (0, 0)),
                  pl.BlockSpec((1, HEAD_COLS), lambda b, h: (0, 0))],
        out_specs=pl.BlockSpec(heads, lambda b, h: (h, b, 0)),
        out_shape=jax.ShapeDtypeStruct((DA_HEADS, T, HEAD_COLS), BF16),
        scratch_shapes=[pltpu.VMEM((hps, 2, S, HEAD_COLS), BF16), pltpu.VMEM((hps, S, HEAD_COLS), BF16)],
        compiler_params=_cparams(("parallel", "parallel")),
        name="diff_attn",
    )(proj, proj, proj, *tables, lam, subln_g)


def _fcum_kernel(f_ref, bf_ref, cum_ref, cumt_ref, *, blk):
    x = f_ref[...] + bf_ref[...]
    logf = jnp.minimum(x, 0.0) - jnp.log1p(jnp.exp(-jnp.abs(x)))
    r = lax.broadcasted_iota(jnp.int32, (blk, blk), 0)
    c = lax.broadcasted_iota(jnp.int32, (blk, blk), 1)
    tri = (r >= c).astype(F32)
    carry = jnp.zeros((1, LANES), F32)
    for n in range(x.shape[0] // blk):
        part = jnp.dot(tri, logf[n * blk:(n + 1) * blk], preferred_element_type=F32,
                       precision=lax.Precision.HIGHEST) + carry
        cum_ref[n * blk:(n + 1) * blk, :] = part
        carry = part[blk - 1:blk, :]
    cum_t = cum_ref[...].T
    for hh in range(FA_HEADS):
        cumt_ref[hh] = cum_t[hh:hh + 1, :]


def _fcum(f_logit, b_f, B, S):
    T = f_logit.shape[0]
    return pl.pallas_call(
        functools.partial(_fcum_kernel, blk=FCUM_BLOCK),
        grid=(B,),
        in_specs=[pl.BlockSpec((S, LANES), lambda b: (b, 0)),
                  pl.BlockSpec((1, LANES), lambda b: (0, 0))],
        out_specs=[pl.BlockSpec((S, LANES), lambda b: (b, 0)),
                   pl.BlockSpec((FA_HEADS, 1, S), lambda b: (b, 0, 0))],
        out_shape=[jax.ShapeDtypeStruct((T, LANES), F32),
                   jax.ShapeDtypeStruct((B * FA_HEADS, 1, S), F32)],
        compiler_params=_cparams(("parallel",)),
        name="forget_cumsum",
    )(f_logit, b_f)


def _fox_attn_kernel(q_ref, k_ref, v_ref, cum_ref, cumt_ref, o_ref, qs_ref, *, tq):
    n_heads, S, _ = q_ref.shape
    causal = _causal(tq, tq)
    for hh in range(n_heads):
        head = pl.program_id(1) * n_heads + hh
        qs_ref[hh] = (q_ref[hh].astype(F32) * (FA_HEAD_DIM ** -0.5 * LOG2E)).astype(BF16)

        def logits(i, hh=hh):
            keys = pl.ds(0, (i + 1) * tq)
            s = _qk(qs_ref[hh, pl.ds(i * tq, tq), :], k_ref[hh, keys, :]) - cumt_ref[hh, :, keys] * LOG2E
            return _mask_last_block(s, causal)

        def weights(i, s, head=head):
            cum = cum_ref[pl.ds(i * tq, tq), :]
            lane = lax.broadcasted_iota(jnp.int32, cum.shape, 1)
            cq = jnp.sum(jnp.where(lane == head, cum, 0.0), axis=-1, keepdims=True) * LOG2E
            p, l = _row_softmax(s, cq)
            return p.astype(BF16), l

        def finish(i, p, l, hh=hh):
            o = jnp.dot(p, v_ref[hh, pl.ds(0, (i + 1) * tq), :], preferred_element_type=F32)
            o_ref[hh, pl.ds(i * tq, tq), :] = (o / l).astype(o_ref.dtype)

        _pipelined_blocks(_long_short_order(S // tq), logits, weights, finish)


def _fox_attn(proj, cum, cum_t, B, S, tq=ATTN_TQ):
    T = proj.shape[1]
    hps = ATTN_HEADS_PER_STEP
    heads = (hps, S, HEAD_COLS)
    first = lambda slot: slot * SLOT_GROUPS // hps
    return pl.pallas_call(
        functools.partial(_fox_attn_kernel, tq=tq),
        grid=(B, FA_HEADS // hps),
        in_specs=[pl.BlockSpec(heads, lambda b, h: (first(J_CQ) + h, b, 0)),
                  pl.BlockSpec(heads, lambda b, h: (first(J_CK) + h, b, 0)),
                  pl.BlockSpec(heads, lambda b, h: (first(J_CV) + h, b, 0)),
                  pl.BlockSpec((S, LANES), lambda b, h: (b, 0)),
                  pl.BlockSpec((hps, 1, S), lambda b, h: (b * (FA_HEADS // hps) + h, 0, 0))],
        out_specs=pl.BlockSpec(heads, lambda b, h: (h, b, 0)),
        out_shape=jax.ShapeDtypeStruct((FA_HEADS, T, HEAD_COLS), BF16),
        scratch_shapes=[pltpu.VMEM((hps, S, HEAD_COLS), BF16)],
        compiler_params=_cparams(("parallel", "parallel")),
        name="fox_attn",
    )(proj, proj, proj, cum, cum_t)


def _pool_kernel(h_ref, w_ref, sc_ref, o_ref):
    S = h_ref.shape[1]
    t = lax.broadcasted_iota(jnp.int32, (S, HEAD_COLS), 0)
    for g, win in enumerate(POOL_WINDOWS):
        cols = slice(g * HEAD_COLS, (g + 1) * HEAD_COLS)
        x = h_ref[g].astype(F32)
        tot = x
        span = 1
        while span < win:
            tot = tot + jnp.where(t >= span, pltpu.roll(tot, span, 0), 0.0)
            span *= 2
        cnt = jnp.minimum(t + 1, win).astype(F32)
        pooled = (tot / cnt - x).astype(BF16)
        y = jnp.dot(pooled, w_ref[g].astype(BF16), preferred_element_type=F32)
        o_ref[g] = (y * sc_ref[:, cols]).astype(o_ref.dtype)


def _pool(proj, w_pool, scale, B, S):
    T = proj.shape[1]
    grp = (SLOT_GROUPS, S, HEAD_COLS)
    return pl.pallas_call(
        _pool_kernel,
        grid=(B,),
        in_specs=[pl.BlockSpec(grp, lambda b: (J_DH, b, 0)),
                  pl.BlockSpec(w_pool.shape, lambda b: (0, 0, 0)),
                  pl.BlockSpec((1, SLOT), lambda b: (0, 0))],
        out_specs=pl.BlockSpec(grp, lambda b: (0, b, 0)),
        out_shape=jax.ShapeDtypeStruct((SLOT_GROUPS, T, HEAD_COLS), BF16),
        compiler_params=_cparams(("parallel",)),
        name="pool",
    )(proj, w_pool, scale)


def _merge_kernel(oa_ref, ob_ref, oc_ref, od_ref, gate_ref, wb_ref, wo_ref, g_ref, h_ref, out_ref):
    def wide(ref, g0, n):
        return jnp.concatenate([ref[g] for g in range(g0, g0 + n)], axis=1)

    per_branch = h_ref.shape[1] // HEAD_COLS
    merged = None
    for n, br in enumerate((oa_ref, ob_ref, oc_ref, od_ref)):
        bd = jnp.dot(wide(br, 0, SLOT_GROUPS), wb_ref[n], preferred_element_type=F32)
        term = jax.nn.sigmoid(wide(gate_ref, n * per_branch, per_branch).astype(F32)) * bd
        merged = term if merged is None else merged + term
    y = jnp.dot(merged.astype(BF16), wo_ref[...], preferred_element_type=F32)
    out_ref[...] = h_ref[...] + _rms(y, g_ref[...])


def _merge(branches, gates, w_branch, w_out, g, h, tm=MERGE_TM):
    T, D = h.shape
    row = lambda i: (i, 0)
    once = pl.Buffered(1)
    return pl.pallas_call(
        _merge_kernel,
        grid=(T // tm,),
        in_specs=[pl.BlockSpec((SLOT_GROUPS, tm, HEAD_COLS), lambda i: (0, i, 0))] * N_BRANCH + [
            pl.BlockSpec((N_BRANCH * D // HEAD_COLS, tm, HEAD_COLS), lambda i: (0, i, 0)),
            pl.BlockSpec(w_branch.shape, lambda i: (0, 0, 0), pipeline_mode=once),
            pl.BlockSpec(w_out.shape, lambda i: (0, 0), pipeline_mode=once),
            pl.BlockSpec((1, D), lambda i: (0, 0)),
            pl.BlockSpec((tm, D), row)],
        out_specs=pl.BlockSpec((tm, D), row),
        out_shape=jax.ShapeDtypeStruct((T, D), F32),
        compiler_params=_cparams(("parallel",), MERGE_VMEM_MB),
        name="merge_out",
    )(*branches, gates, w_branch, w_out, g, h)


def _ffn_kernel(h_ref, g1_ref, wu_ref, wd_ref, g2_ref, out_ref, hn_ref, acc_ref):
    f = pl.program_id(1)

    @pl.when(f == 0)
    def _():
        hn_ref[...] = _rms(h_ref[...], g1_ref[...]).astype(BF16)
        acc_ref[...] = jnp.zeros_like(acc_ref)

    up = jnp.dot(hn_ref[...], wu_ref[...], preferred_element_type=F32)
    a = jnp.square(jnp.maximum(up, 0.0)).astype(BF16)
    acc_ref[...] += jnp.dot(a, wd_ref[...], preferred_element_type=F32)

    @pl.when(f == pl.num_programs(1) - 1)
    def _():
        g2 = g2_ref[...]
        for r in _row_chunks(h_ref.shape[0]):
            out_ref[r, :] = h_ref[r, :] + _rms(acc_ref[r, :], g2)


def _ffn(h, g1, w_up, w_down, g2, tm=FFN_TM, tf=FFN_TF):
    T, D = h.shape
    F = w_up.shape[1]
    row = lambda i, f: (i, 0)
    vec = lambda i, f: (0, 0)
    return pl.pallas_call(
        _ffn_kernel,
        grid=(T // tm, F // tf),
        in_specs=[pl.BlockSpec((tm, D), row),
                  pl.BlockSpec((1, D), vec),
                  pl.BlockSpec((D, tf), lambda i, f: (0, f)),
                  pl.BlockSpec((tf, D), lambda i, f: (f, 0)),
                  pl.BlockSpec((1, D), vec)],
        out_specs=pl.BlockSpec((tm, D), row),
        out_shape=jax.ShapeDtypeStruct((T, D), F32),
        scratch_shapes=[pltpu.VMEM((tm, D), BF16), pltpu.VMEM((tm, D), F32)],
        compiler_params=_cparams(("parallel", "arbitrary"), FFN_VMEM_MB),
        name="ffn",
    )(h, g1, w_up, w_down, g2)


def kernel(x, positions, norm_mix_pre, norm_mix_post, norm_ffn_pre, norm_ffn_post, w_in, gm_ln_g,
           gm_ln_b, gm_w_s, gm_b_s, da_lambda, da_subln_g, fa_b_f, pool_w, pool_scale, w_branch,
           w_out, w_ffn_up, w_ffn_down):
    B, S, D = x.shape
    T = B * S
    h = x.reshape(T, D)
    tables = _rope_tables(positions)
    w_t, wf_t = _prep_w_in(w_in)
    L, NB, W, _ = w_branch.shape
    to_cast = (w_branch.reshape(L, NB * W, D), w_out, w_ffn_up, w_ffn_down)
    row = lambda a: a.reshape(1, -1)
    for l in range(DEPTH):
        lambda_init = 0.8 - 0.6 * math.exp(-0.3 * l)
        proj, f_logit, (wb, wo, wu, wd) = _inproj(h, row(norm_mix_pre[l]), w_t, wf_t, l, to_cast)

        o_a = _gmlp(proj, row(gm_ln_g[l]), row(gm_ln_b[l]), gm_w_s[l], gm_b_s[l].T)
        o_b = _diff_attn(proj, tables, da_lambda[l], row(da_subln_g[l]), B, S, lambda_init)
        b_f = jnp.pad(fa_b_f[l], (0, LANES - FA_HEADS)).reshape(1, LANES)
        cum, cum_t = _fcum(f_logit, b_f, B, S)
        o_c = _fox_attn(proj, cum, cum_t, B, S)
        o_d = _pool(proj, pool_w[l], row(pool_scale[l]), B, S)

        h = _merge((o_a, o_b, o_c, o_d), proj, wb.reshape(NB, W, D), wo, row(norm_mix_post[l]), h)
        h = _ffn(h, row(norm_ffn_pre[l]), wu, wd, row(norm_ffn_post[l]))
    return h.reshape(B, S, D)
```

```python
import functools
import math

import jax
import jax.numpy as jnp
from jax import lax
from jax.experimental import pallas as pl
from jax.experimental.pallas import tpu as pltpu

F32 = jnp.float32
BF16 = jnp.bfloat16

DEPTH = 2
N_BRANCH = 4
GM_GROUPS = 4
GM_CHUNK = 128
DA_HEADS = 4
DA_QK_DIM = 64
DA_ROT_DIM = 16
FA_HEADS = 4
FA_HEAD_DIM = 128
POOL_WINDOWS = (2, 4, 8, 16)
ROPE_THETA = 500000.0
NORM_EPS = 1e-6
MASK_VALUE = -1e30
LOG2E = math.log2(math.e)

LANES = 128
SUBLANES = 8
ROW_CHUNK = 2 * SUBLANES
CAST_BLOCKS = 64
HEAD_COLS = 128
SLOT = 512
SLOT_GROUPS = SLOT // HEAD_COLS
ATTN_HEADS_PER_STEP = 2

ROPE_TM = 1024
PREP_VMEM_MB = 40
INPROJ_TM, INPROJ_TN, INPROJ_VMEM_MB = 1024, 1280, 56
GMLP_TM = 2048
ATTN_TQ = 256
FCUM_BLOCK = 256
MERGE_TM, MERGE_VMEM_MB = 256, 48
FFN_TM, FFN_TF, FFN_VMEM_MB = 512, 1024, 48


N_GATE_SLOTS = 16
J_U, J_V, J_BQ, J_BK, J_BV, J_CQ, J_CK, J_CV, J_DH = range(N_GATE_SLOTS, N_GATE_SLOTS + 9)
N_SLOTS = J_DH + 1
N_ALIGNED = 8


def _rms(x, g):
    return x * lax.rsqrt(jnp.mean(x * x, axis=-1, keepdims=True) + NORM_EPS) * g


def _row_chunks(n_rows, rows=ROW_CHUNK):
    return [pl.ds(r0, rows) for r0 in range(0, n_rows, rows)]


def _next_row_tile(x_hbm, buf, sem):
    i, n = pl.program_id(0), pl.num_programs(0)
    tm = buf.shape[1]

    def copy(tile, slot):
        rows = pl.ds(pl.multiple_of(tile * tm, tm), tm)
        return pltpu.make_async_copy(x_hbm.at[rows, :], buf.at[slot], sem.at[slot])

    slot = i % 2

    @pl.when(i == 0)
    def _():
        copy(0, 0).start()

    copy(i, slot).wait()

    @pl.when(i + 1 < n)
    def _():
        copy(i + 1, 1 - slot).start()

    return slot


def _cparams(sem, vmem_mb=None):
    kw = dict(dimension_semantics=sem)
    if vmem_mb is not None:
        kw["vmem_limit_bytes"] = vmem_mb * 1024 * 1024
    return pltpu.CompilerParams(**kw)


def _rope_kernel(pos_ref, inv_ref, c_ref, sa_ref, sb_ref):
    ang = pos_ref[...].astype(F32) * inv_ref[...]
    sub = lax.broadcasted_iota(jnp.int32, ang.shape, 1) % DA_QK_DIM
    half = DA_ROT_DIM // 2
    s = jnp.sin(ang)
    c_ref[...] = jnp.cos(ang)
    sa_ref[...] = jnp.where(sub < half, -s, 0.0)
    sb_ref[...] = jnp.where((sub >= half) & (sub < 2 * half), s, 0.0)


def _rope_tables(positions):
    T = positions.size
    tm = ROPE_TM
    half = DA_ROT_DIM // 2
    inv = 1.0 / (ROPE_THETA ** (jnp.arange(0, DA_ROT_DIM, 2, dtype=F32) / DA_ROT_DIM))
    sub = jnp.arange(LANES) % DA_QK_DIM
    inv_lane = jnp.where(sub < 2 * half, inv[sub % half], 0.0).reshape(1, LANES)
    tab = jax.ShapeDtypeStruct((T, LANES), F32)
    return pl.pallas_call(
        _rope_kernel,
        grid=(T // tm,),
        in_specs=[pl.BlockSpec((tm, 1), lambda i: (i, 0)),
                  pl.BlockSpec((1, LANES), lambda i: (0, 0))],
        out_specs=[pl.BlockSpec((tm, LANES), lambda i: (i, 0))] * 3,
        out_shape=[tab] * 3,
        compiler_params=_cparams(("parallel",)),
        name="rope_tables",
    )(positions.reshape(T, 1), inv_lane)


def _rotate(x, c, sa, sb):
    half = DA_ROT_DIM // 2
    return x * c + pltpu.roll(x, LANES - half, 1) * sa + pltpu.roll(x, half, 1) * sb


def _wprep_kernel(a_ref, o_ref):
    n_layers, tn, _ = o_ref.shape
    sub = lax.broadcasted_iota(jnp.int32, (SUBLANES, LANES), 0)
    low = {d: (sub & d) == 0 for d in (4, 2, 1)}

    def transpose8(a):
        for d in (4, 2, 1):
            nxt = list(a)
            for i in range(SUBLANES):
                if not i & d:
                    x, y = a[i], a[i + d]
                    nxt[i] = jnp.where(low[d], x, pltpu.roll(y, d, 0))
                    nxt[i + d] = jnp.where(low[d], pltpu.roll(x, SUBLANES - d, 0), y)
            a = nxt
        return a

    rows_per_iter = 2 * SUBLANES

    def body(it, carry):
        n0 = pl.multiple_of(it * rows_per_iter, rows_per_iter)
        for jb in range(a_ref.shape[1] // SUBLANES):
            js = pl.ds(jb * SUBLANES, SUBLANES)
            halves = [transpose8([a_ref[n0 + half * SUBLANES + i, js, :] for i in range(SUBLANES)])
                      for half in range(2)]
            for s in range(SUBLANES):
                kb, l = divmod(jb * SUBLANES + s, n_layers)
                tile = jnp.concatenate([halves[0][s], halves[1][s]], axis=0)
                o_ref[l, pl.ds(n0, rows_per_iter), kb * LANES:(kb + 1) * LANES] = tile.astype(o_ref.dtype)
        return carry

    lax.fori_loop(0, tn // rows_per_iter, body, 0)


def _native_rows(w_in):
    L, D, N = w_in.shape
    v = jnp.transpose(w_in.reshape(L, D // LANES, LANES, N), (3, 1, 0, 2))
    return v.reshape(N, (D // LANES) * L, LANES)


def _prep_rows(w_rows, L, D, tn, n, offset):
    return pl.pallas_call(
        _wprep_kernel,
        grid=(n,),
        in_specs=[pl.BlockSpec((pl.Element(tn), pl.Element(w_rows.shape[1]), pl.Element(LANES)),
                               lambda j: (offset(j), 0, 0))],
        out_specs=pl.BlockSpec((L, tn, D), lambda j: (0, j, 0)),
        out_shape=jax.ShapeDtypeStruct((L, n * tn, D), BF16),
        compiler_params=_cparams(("parallel",), PREP_VMEM_MB),
        name="w_in_prep",
    )(w_rows)


def _prep_w_in(w_in):
    L, D, _ = w_in.shape
    f0 = N_ALIGNED * SLOT
    d0 = f0 + FA_HEADS
    g0 = d0 + SLOT

    def slot_row(j):
        return jnp.where(j < N_GATE_SLOTS, g0 + SLOT * j, jnp.where(j == J_DH, d0, SLOT * (j - N_GATE_SLOTS)))

    w_rows = _native_rows(w_in)
    return (_prep_rows(w_rows, L, D, SLOT, N_SLOTS, slot_row),
            _prep_rows(w_rows, L, D, LANES, 1, lambda j: f0 + 0 * j))


def _dot_nt(a, b):
    return lax.dot_general(a, b, (((1,), (1,)), ((), ())), preferred_element_type=F32)


def _inproj_kernel(*refs, n_cast):
    h_hbm, g_ref, w_ref, wf_ref = refs[:4]
    cast_in = refs[4:4 + n_cast]
    out_ref, f_ref = refs[4 + n_cast:6 + n_cast]
    cast_out = refs[6 + n_cast:6 + 2 * n_cast]
    xn_ref, h_buf, h_sem = refs[-3:]

    @pl.when(pl.program_id(1) == 0)
    def _():
        slot = _next_row_tile(h_hbm, h_buf, h_sem)
        xn_ref[...] = _rms(h_buf[slot], g_ref[...]).astype(BF16)
        f_ref[...] = _dot_nt(xn_ref[...], wf_ref[...])

    r = _dot_nt(xn_ref[...], w_ref[...])
    for c in range(out_ref.shape[0]):
        out_ref[c] = r[:, c * LANES:(c + 1) * LANES].astype(out_ref.dtype)
    for src, dst in zip(cast_in, cast_out):
        dst[...] = src[...].astype(dst.dtype)


def _inproj(h, g, w_t, wf_t, l, to_cast, tm=INPROJ_TM, tn=INPROJ_TN):
    T, D = h.shape
    N = w_t.shape[1]
    ni, nj = T // tm, N // tn
    n_blk = CAST_BLOCKS
    assert ni * nj >= n_blk
    row = lambda i, j: (i, 0)
    blk = lambda i, j: jnp.minimum(i * nj + j, n_blk - 1)
    cast_in, cast_out, cast_shape = [], [], []
    for w in to_cast:
        _, R, C = w.shape
        cast_in.append(pl.BlockSpec((None, R // n_blk, C), lambda i, j: (l, blk(i, j), 0)))
        cast_out.append(pl.BlockSpec((R // n_blk, C), lambda i, j: (blk(i, j), 0)))
        cast_shape.append(jax.ShapeDtypeStruct((R, C), BF16))
    res = pl.pallas_call(
        functools.partial(_inproj_kernel, n_cast=len(to_cast)),
        grid=(ni, nj),
        in_specs=[pl.BlockSpec(memory_space=pl.ANY),
                  pl.BlockSpec((1, D), lambda i, j: (0, 0)),
                  pl.BlockSpec((None, tn, D), lambda i, j: (l, j, 0)),
                  pl.BlockSpec((None, LANES, D), lambda i, j: (l, 0, 0))] + cast_in,
        out_specs=[pl.BlockSpec((tn // LANES, tm, LANES), lambda i, j: (j, i, 0)),
                   pl.BlockSpec((tm, LANES), row)] + cast_out,
        out_shape=[jax.ShapeDtypeStruct((N // LANES, T, LANES), BF16),
                   jax.ShapeDtypeStruct((T, LANES), F32)] + cast_shape,
        scratch_shapes=[pltpu.VMEM((tm, D), BF16), pltpu.VMEM((2, tm, D), F32), pltpu.SemaphoreType.DMA((2,))],
        compiler_params=_cparams(("arbitrary", "arbitrary"), INPROJ_VMEM_MB),
        name="inproj",
    )(h, g, w_t, wf_t, *to_cast)
    return res[0], res[1], res[2:]


def _gmlp_kernel(u_ref, v_ref, lng_ref, lnb_ref, ws_ref, bst_ref, o_ref):
    v = jnp.concatenate([v_ref[g] for g in range(GM_GROUPS)], axis=1).astype(F32)
    mu = jnp.mean(v, axis=-1, keepdims=True)
    vc = v - mu
    var = jnp.mean(vc * vc, axis=-1, keepdims=True)
    vn = (vc * lax.rsqrt(var + NORM_EPS) * lng_ref[...] + lnb_ref[...]).astype(BF16)
    r = lax.broadcasted_iota(jnp.int32, (GM_CHUNK, GM_CHUNK), 0)
    c = lax.broadcasted_iota(jnp.int32, (GM_CHUNK, GM_CHUNK), 1)
    causal = r >= c
    for g in range(GM_GROUPS):
        wg = jnp.where(causal, ws_ref[g], 0.0).astype(BF16)
        bcol = bst_ref[:, g:g + 1]
        cols = slice(g * HEAD_COLS, (g + 1) * HEAD_COLS)
        for n in range(v.shape[0] // GM_CHUNK):
            rows = slice(n * GM_CHUNK, (n + 1) * GM_CHUNK)
            mixed = jnp.dot(wg, vn[rows, cols], preferred_element_type=F32) + bcol
            o_ref[g, rows, :] = (u_ref[g, rows, :].astype(F32) * mixed).astype(o_ref.dtype)


def _gmlp(proj, ln_g, ln_b, w_s, b_s_t, tm=GMLP_TM):
    T = proj.shape[1]
    const2 = lambda i: (0, 0)
    grp = (SLOT_GROUPS, tm, HEAD_COLS)
    return pl.pallas_call(
        _gmlp_kernel,
        grid=(T // tm,),
        in_specs=[pl.BlockSpec(grp, lambda i: (J_U, i, 0)),
                  pl.BlockSpec(grp, lambda i: (J_V, i, 0)),
                  pl.BlockSpec((1, SLOT), const2),
                  pl.BlockSpec((1, SLOT), const2),
                  pl.BlockSpec((GM_GROUPS, GM_CHUNK, GM_CHUNK), lambda i: (0, 0, 0)),
                  pl.BlockSpec((GM_CHUNK, GM_GROUPS), const2)],
        out_specs=pl.BlockSpec(grp, lambda i: (0, i, 0)),
        out_shape=jax.ShapeDtypeStruct((SLOT_GROUPS, T, HEAD_COLS), BF16),
        compiler_params=_cparams(("parallel",)),
        name="gmlp",
    )(proj, proj, ln_g, ln_b, w_s, b_s_t)


def _qk(q, k):
    return lax.dot_general(q, k, (((1,), (1,)), ((), ())), preferred_element_type=F32)


def _causal(rows, tq):
    r = lax.broadcasted_iota(jnp.int32, (rows, tq), 0) % tq
    c = lax.broadcasted_iota(jnp.int32, (rows, tq), 1)
    return r >= c


def _long_short_order(n):
    lo, hi, out = 0, n - 1, []
    while lo <= hi:
        out.append(hi)
        hi -= 1
        if lo <= hi:
            out.append(lo)
            lo += 1
    return out


def _pipelined_blocks(order, logits, weights, finish):
    assert len(order) >= 2
    s_q = [logits(i) for i in order[:2]]
    w_q = [weights(order[0], s_q.pop(0))]
    for n, i in enumerate(order):
        if n + 2 < len(order):
            s_q.append(logits(order[n + 2]))
        if n + 1 < len(order):
            w_q.append(weights(order[n + 1], s_q.pop(0)))
        finish(i, *w_q.pop(0))


def _mask_last_block(s, causal):
    tq = causal.shape[1]
    diag = jnp.where(causal, s[:, -tq:], MASK_VALUE)
    return diag if s.shape[1] == tq else jnp.concatenate([s[:, :-tq], diag], axis=1)


def _row_softmax(s, row_const=None):
    m = jnp.max(s, axis=-1, keepdims=True)
    if row_const is not None:
        m = (m + row_const) - row_const
    p = jnp.exp2(s - m)
    return p, jnp.sum(p, axis=-1, keepdims=True)


def _diff_attn_kernel(q_ref, k_ref, v_ref, c_ref, sa_ref, sb_ref, lam_ref, g_ref, o_ref,
                      qs_ref, kr_ref, *, tq, lambda_init):
    n_heads, S, _ = q_ref.shape
    c, sa, sb = c_ref[...], sa_ref[...], sb_ref[...]
    lp = lam_ref[...]

    def total(x):
        return jnp.sum(jnp.sum(x, axis=1, keepdims=True), axis=0, keepdims=True)

    lam = jnp.exp(total(lp[0:1] * lp[1:2])) - jnp.exp(total(lp[2:3] * lp[3:4])) + lambda_init
    causal = _causal(2 * tq, tq)
    g = g_ref[...]
    lane = lax.broadcasted_iota(jnp.int32, (S, HEAD_COLS), 1)

    for hh in range(n_heads):
        qf = _rotate(q_ref[hh].astype(F32), c, sa, sb) * (DA_QK_DIM ** -0.5 * LOG2E)
        qs_ref[hh, 0] = jnp.where(lane < DA_QK_DIM, qf, 0.0).astype(BF16)
        qs_ref[hh, 1] = jnp.where(lane >= DA_QK_DIM, qf, 0.0).astype(BF16)
        kr_ref[hh] = _rotate(k_ref[hh].astype(F32), c, sa, sb).astype(BF16)

        def logits(i, hh=hh):
            rows = pl.ds(i * tq, tq)
            q2 = jnp.concatenate([qs_ref[hh, 0, rows, :], qs_ref[hh, 1, rows, :]], axis=0)
            return _mask_last_block(_qk(q2, kr_ref[hh, pl.ds(0, (i + 1) * tq), :]), causal)

        def weights(i, s):
            p, l = _row_softmax(s)
            ratio = lam * l[:tq] / l[tq:]
            return (p[:tq] - ratio * p[tq:]).astype(BF16), l[:tq]

        def finish(i, w, l1, hh=hh):
            o = jnp.dot(w, v_ref[hh, pl.ds(0, (i + 1) * tq), :], preferred_element_type=F32) / l1
            o_ref[hh, pl.ds(i * tq, tq), :] = (_rms(o, g) * (1.0 - lambda_init)).astype(o_ref.dtype)

        _pipelined_blocks(_long_short_order(S // tq), logits, weights, finish)


def _diff_attn(proj, tables, lam, subln_g, B, S, lambda_init, tq=ATTN_TQ):
    T = proj.shape[1]
    hps = ATTN_HEADS_PER_STEP
    seq = lambda b, h: (b, 0)
    heads = (hps, S, HEAD_COLS)
    first = lambda slot: slot * SLOT_GROUPS // hps
    kern = functools.partial(_diff_attn_kernel, tq=tq, lambda_init=lambda_init)
    return pl.pallas_call(
        kern,
        grid=(B, DA_HEADS // hps),
        in_specs=[pl.BlockSpec(heads, lambda b, h: (first(J_BQ) + h, b, 0)),
                  pl.BlockSpec(heads, lambda b, h: (first(J_BK) + h, b, 0)),
                  pl.BlockSpec(heads, lambda b, h: (first(J_BV) + h, b, 0)),
                  pl.BlockSpec((S, LANES), seq),
                  pl.BlockSpec((S, LANES), seq),
                  pl.BlockSpec((S, LANES), seq),
                  pl.BlockSpec(lam.shape, lambda b, h: (0, 0)),
                  pl.BlockSpec((1, HEAD_COLS), lambda b, h: (0, 0))],
        out_specs=pl.BlockSpec(heads, lambda b, h: (h, b, 0)),
        out_shape=jax.ShapeDtypeStruct((DA_HEADS, T, HEAD_COLS), BF16),
        scratch_shapes=[pltpu.VMEM((hps, 2, S, HEAD_COLS), BF16), pltpu.VMEM((hps, S, HEAD_COLS), BF16)],
        compiler_params=_cparams(("parallel", "parallel")),
        name="diff_attn",
    )(proj, proj, proj, *tables, lam, subln_g)


def _fcum_kernel(f_ref, bf_ref, cum_ref, cumt_ref, *, blk):
    x = f_ref[...] + bf_ref[...]
    logf = jnp.minimum(x, 0.0) - jnp.log1p(jnp.exp(-jnp.abs(x)))
    r = lax.broadcasted_iota(jnp.int32, (blk, blk), 0)
    c = lax.broadcasted_iota(jnp.int32, (blk, blk), 1)
    tri = (r >= c).astype(F32)
    carry = jnp.zeros((1, LANES), F32)
    for n in range(x.shape[0] // blk):
        part = jnp.dot(tri, logf[n * blk:(n + 1) * blk], preferred_element_type=F32,
                       precision=lax.Precision.HIGHEST) + carry
        cum_ref[n * blk:(n + 1) * blk, :] = part
        carry = part[blk - 1:blk, :]
    cum_t = cum_ref[...].T
    for hh in range(FA_HEADS):
        cumt_ref[hh] = cum_t[hh:hh + 1, :]


def _fcum(f_logit, b_f, B, S):
    T = f_logit.shape[0]
    return pl.pallas_call(
        functools.partial(_fcum_kernel, blk=FCUM_BLOCK),
        grid=(B,),
        in_specs=[pl.BlockSpec((S, LANES), lambda b: (b, 0)),
                  pl.BlockSpec((1, LANES), lambda b: (0, 0))],
        out_specs=[pl.BlockSpec((S, LANES), lambda b: (b, 0)),
                   pl.BlockSpec((FA_HEADS, 1, S), lambda b: (b, 0, 0))],
        out_shape=[jax.ShapeDtypeStruct((T, LANES), F32),
                   jax.ShapeDtypeStruct((B * FA_HEADS, 1, S), F32)],
        compiler_params=_cparams(("parallel",)),
        name="forget_cumsum",
    )(f_logit, b_f)


def _fox_attn_kernel(q_ref, k_ref, v_ref, cum_ref, cumt_ref, o_ref, qs_ref, *, tq):
    n_heads, S, _ = q_ref.shape
    causal = _causal(tq, tq)
    for hh in range(n_heads):
        head = pl.program_id(1) * n_heads + hh
        qs_ref[hh] = (q_ref[hh].astype(F32) * (FA_HEAD_DIM ** -0.5 * LOG2E)).astype(BF16)

        def logits(i, hh=hh):
            keys = pl.ds(0, (i + 1) * tq)
            s = _qk(qs_ref[hh, pl.ds(i * tq, tq), :], k_ref[hh, keys, :]) - cumt_ref[hh, :, keys] * LOG2E
            return _mask_last_block(s, causal)

        def weights(i, s, head=head):
            cum = cum_ref[pl.ds(i * tq, tq), :]
            lane = lax.broadcasted_iota(jnp.int32, cum.shape, 1)
            cq = jnp.sum(jnp.where(lane == head, cum, 0.0), axis=-1, keepdims=True) * LOG2E
            p, l = _row_softmax(s, cq)
            return p.astype(BF16), l

        def finish(i, p, l, hh=hh):
            o = jnp.dot(p, v_ref[hh, pl.ds(0, (i + 1) * tq), :], preferred_element_type=F32)
            o_ref[hh, pl.ds(i * tq, tq), :] = (o / l).astype(o_ref.dtype)

        _pipelined_blocks(_long_short_order(S // tq), logits, weights, finish)


def _fox_attn(proj, cum, cum_t, B, S, tq=ATTN_TQ):
    T = proj.shape[1]
    hps = ATTN_HEADS_PER_STEP
    heads = (hps, S, HEAD_COLS)
    first = lambda slot: slot * SLOT_GROUPS // hps
    return pl.pallas_call(
        functools.partial(_fox_attn_kernel, tq=tq),
        grid=(B, FA_HEADS // hps),
        in_specs=[pl.BlockSpec(heads, lambda b, h: (first(J_CQ) + h, b, 0)),
                  pl.BlockSpec(heads, lambda b, h: (first(J_CK) + h, b, 0)),
                  pl.BlockSpec(heads, lambda b, h: (first(J_CV) + h, b, 0)),
                  pl.BlockSpec((S, LANES), lambda b, h: (b, 0)),
                  pl.BlockSpec((hps, 1, S), lambda b, h: (b * (FA_HEADS // hps) + h, 0, 0))],
        out_specs=pl.BlockSpec(heads, lambda b, h: (h, b, 0)),
        out_shape=jax.ShapeDtypeStruct((FA_HEADS, T, HEAD_COLS), BF16),
        scratch_shapes=[pltpu.VMEM((hps, S, HEAD_COLS), BF16)],
        compiler_params=_cparams(("parallel", "parallel")),
        name="fox_attn",
    )(proj, proj, proj, cum, cum_t)


def _pool_kernel(h_ref, w_ref, sc_ref, o_ref):
    S = h_ref.shape[1]
    t = lax.broadcasted_iota(jnp.int32, (S, HEAD_COLS), 0)
    for g, win in enumerate(POOL_WINDOWS):
        cols = slice(g * HEAD_COLS, (g + 1) * HEAD_COLS)
        x = h_ref[g].astype(F32)
        tot = x
        span = 1
        while span < win:
            tot = tot + jnp.where(t >= span, pltpu.roll(tot, span, 0), 0.0)
            span *= 2
        cnt = jnp.minimum(t + 1, win).astype(F32)
        pooled = (tot / cnt - x).astype(BF16)
        y = jnp.dot(pooled, w_ref[g].astype(BF16), preferred_element_type=F32)
        o_ref[g] = (y * sc_ref[:, cols]).astype(o_ref.dtype)


def _pool(proj, w_pool, scale, B, S):
    T = proj.shape[1]
    grp = (SLOT_GROUPS, S, HEAD_COLS)
    return pl.pallas_call(
        _pool_kernel,
        grid=(B,),
        in_specs=[pl.BlockSpec(grp, lambda b: (J_DH, b, 0)),
                  pl.BlockSpec(w_pool.shape, lambda b: (0, 0, 0)),
                  pl.BlockSpec((1, SLOT), lambda b: (0, 0))],
        out_specs=pl.BlockSpec(grp, lambda b: (0, b, 0)),
        out_shape=jax.ShapeDtypeStruct((SLOT_GROUPS, T, HEAD_COLS), BF16),
        compiler_params=_cparams(("parallel",)),
        name="pool",
    )(proj, w_pool, scale)


def _merge_kernel(oa_ref, ob_ref, oc_ref, od_ref, gate_ref, wb_ref, wo_ref, g_ref, h_ref, out_ref):
    def wide(ref, g0, n):
        return jnp.concatenate([ref[g] for g in range(g0, g0 + n)], axis=1)

    per_branch = h_ref.shape[1] // HEAD_COLS
    merged = None
    for n, br in enumerate((oa_ref, ob_ref, oc_ref, od_ref)):
        bd = jnp.dot(wide(br, 0, SLOT_GROUPS), wb_ref[n], preferred_element_type=F32)
        term = jax.nn.sigmoid(wide(gate_ref, n * per_branch, per_branch).astype(F32)) * bd
        merged = term if merged is None else merged + term
    y = jnp.dot(merged.astype(BF16), wo_ref[...], preferred_element_type=F32)
    out_ref[...] = h_ref[...] + _rms(y, g_ref[...])


def _merge(branches, gates, w_branch, w_out, g, h, tm=MERGE_TM):
    T, D = h.shape
    row = lambda i: (i, 0)
    once = pl.Buffered(1)
    return pl.pallas_call(
        _merge_kernel,
        grid=(T // tm,),
        in_specs=[pl.BlockSpec((SLOT_GROUPS, tm, HEAD_COLS), lambda i: (0, i, 0))] * N_BRANCH + [
            pl.BlockSpec((N_BRANCH * D // HEAD_COLS, tm, HEAD_COLS), lambda i: (0, i, 0)),
            pl.BlockSpec(w_branch.shape, lambda i: (0, 0, 0), pipeline_mode=once),
            pl.BlockSpec(w_out.shape, lambda i: (0, 0), pipeline_mode=once),
            pl.BlockSpec((1, D), lambda i: (0, 0)),
            pl.BlockSpec((tm, D), row)],
        out_specs=pl.BlockSpec((tm, D), row),
        out_shape=jax.ShapeDtypeStruct((T, D), F32),
        compiler_params=_cparams(("parallel",), MERGE_VMEM_MB),
        name="merge_out",
    )(*branches, gates, w_branch, w_out, g, h)


def _ffn_kernel(h_hbm, g1_ref, wu_ref, wd_ref, g2_ref, out_ref, hn_ref, acc_ref, h_buf, h_sem):
    f = pl.program_id(1)

    @pl.when(f == 0)
    def _():
        slot = _next_row_tile(h_hbm, h_buf, h_sem)
        hn_ref[...] = _rms(h_buf[slot], g1_ref[...]).astype(BF16)
        acc_ref[...] = jnp.zeros_like(acc_ref)

    up = jnp.dot(hn_ref[...], wu_ref[...], preferred_element_type=F32)
    a = jnp.square(jnp.maximum(up, 0.0)).astype(BF16)
    acc_ref[...] += jnp.dot(a, wd_ref[...], preferred_element_type=F32)

    @pl.when(f == pl.num_programs(1) - 1)
    def _():
        g2 = g2_ref[...]
        slot = pl.program_id(0) % 2
        for r in _row_chunks(out_ref.shape[0]):
            out_ref[r, :] = h_buf[slot, r, :] + _rms(acc_ref[r, :], g2)


def _ffn(h, g1, w_up, w_down, g2, tm=FFN_TM, tf=FFN_TF):
    T, D = h.shape
    F = w_up.shape[1]
    row = lambda i, f: (i, 0)
    vec = lambda i, f: (0, 0)
    return pl.pallas_call(
        _ffn_kernel,
        grid=(T // tm, F // tf),
        in_specs=[pl.BlockSpec(memory_space=pl.ANY),
                  pl.BlockSpec((1, D), vec),
                  pl.BlockSpec((D, tf), lambda i, f: (0, f)),
                  pl.BlockSpec((tf, D), lambda i, f: (f, 0)),
                  pl.BlockSpec((1, D), vec)],
        out_specs=pl.BlockSpec((tm, D), row),
        out_shape=jax.ShapeDtypeStruct((T, D), F32),
        scratch_shapes=[pltpu.VMEM((tm, D), BF16), pltpu.VMEM((tm, D), F32), pltpu.VMEM((2, tm, D), F32),
                        pltpu.SemaphoreType.DMA((2,))],
        compiler_params=_cparams(("arbitrary", "arbitrary"), FFN_VMEM_MB),
        name="ffn",
    )(h, g1, w_up, w_down, g2)


def kernel(x, positions, norm_mix_pre, norm_mix_post, norm_ffn_pre, norm_ffn_post, w_in, gm_ln_g,
           gm_ln_b, gm_w_s, gm_b_s, da_lambda, da_subln_g, fa_b_f, pool_w, pool_scale, w_branch,
           w_out, w_ffn_up, w_ffn_down):
    B, S, D = x.shape
    T = B * S
    h = x.reshape(T, D)
    tables = _rope_tables(positions)
    w_t, wf_t = _prep_w_in(w_in)
    L, NB, W, _ = w_branch.shape
    to_cast = (w_branch.reshape(L, NB * W, D), w_out, w_ffn_up, w_ffn_down)
    row = lambda a: a.reshape(1, -1)
    for l in range(DEPTH):
        lambda_init = 0.8 - 0.6 * math.exp(-0.3 * l)
        proj, f_logit, (wb, wo, wu, wd) = _inproj(h, row(norm_mix_pre[l]), w_t, wf_t, l, to_cast)

        o_a = _gmlp(proj, row(gm_ln_g[l]), row(gm_ln_b[l]), gm_w_s[l], gm_b_s[l].T)
        o_b = _diff_attn(proj, tables, da_lambda[l], row(da_subln_g[l]), B, S, lambda_init)
        b_f = jnp.pad(fa_b_f[l], (0, LANES - FA_HEADS)).reshape(1, LANES)
        cum, cum_t = _fcum(f_logit, b_f, B, S)
        o_c = _fox_attn(proj, cum, cum_t, B, S)
        o_d = _pool(proj, pool_w[l], row(pool_scale[l]), B, S)

        h = _merge((o_a, o_b, o_c, o_d), proj, wb.reshape(NB, W, D), wo, row(norm_mix_post[l]), h)
        h = _ffn(h, row(norm_ffn_pre[l]), wu, wd, row(norm_ffn_post[l]))
    return h.reshape(B, S, D)
```

```python
import functools
import math

import jax
import jax.numpy as jnp
from jax import lax
from jax.experimental import pallas as pl
from jax.experimental.pallas import tpu as pltpu

F32 = jnp.float32
BF16 = jnp.bfloat16

DEPTH = 2
N_BRANCH = 4
GM_GROUPS = 4
GM_CHUNK = 128
DA_HEADS = 4
DA_QK_DIM = 64
DA_ROT_DIM = 16
FA_HEADS = 4
FA_HEAD_DIM = 128
POOL_WINDOWS = (2, 4, 8, 16)
ROPE_THETA = 500000.0
NORM_EPS = 1e-6
MASK_VALUE = -1e30
LOG2E = math.log2(math.e)

LANES = 128
SUBLANES = 8
ROW_CHUNK = 2 * SUBLANES
CAST_BLOCKS = 64
HEAD_COLS = 128
SLOT = 512
SLOT_GROUPS = SLOT // HEAD_COLS
ATTN_HEADS_PER_STEP = 2

ROPE_TM = 1024
PREP_VMEM_MB = 40
INPROJ_TM, INPROJ_TN, INPROJ_VMEM_MB = 1024, 1280, 56
INPROJ_W_SLOTS = 3
GMLP_TM = 2048
ATTN_TQ = 256
FCUM_BLOCK = 256
MERGE_TM, MERGE_VMEM_MB = 256, 48
FFN_TM, FFN_TF, FFN_VMEM_MB = 512, 1024, 48


N_GATE_SLOTS = 16
J_U, J_V, J_BQ, J_BK, J_BV, J_CQ, J_CK, J_CV, J_DH = range(N_GATE_SLOTS, N_GATE_SLOTS + 9)
N_SLOTS = J_DH + 1
N_ALIGNED = 8


def _rms(x, g):
    return x * lax.rsqrt(jnp.mean(x * x, axis=-1, keepdims=True) + NORM_EPS) * g


def _row_chunks(n_rows, rows=ROW_CHUNK):
    return [pl.ds(r0, rows) for r0 in range(0, n_rows, rows)]


def _next_row_tile(x_hbm, buf, sem):
    i, n = pl.program_id(0), pl.num_programs(0)
    tm = buf.shape[1]

    def copy(tile, slot):
        rows = pl.ds(pl.multiple_of(tile * tm, tm), tm)
        return pltpu.make_async_copy(x_hbm.at[rows, :], buf.at[slot], sem.at[slot])

    slot = i % 2

    @pl.when(i == 0)
    def _():
        copy(0, 0).start()

    copy(i, slot).wait()

    @pl.when(i + 1 < n)
    def _():
        copy(i + 1, 1 - slot).start()

    return slot


def _cparams(sem, vmem_mb=None):
    kw = dict(dimension_semantics=sem)
    if vmem_mb is not None:
        kw["vmem_limit_bytes"] = vmem_mb * 1024 * 1024
    return pltpu.CompilerParams(**kw)


def _rope_kernel(pos_ref, inv_ref, c_ref, sa_ref, sb_ref):
    ang = pos_ref[...].astype(F32) * inv_ref[...]
    sub = lax.broadcasted_iota(jnp.int32, ang.shape, 1) % DA_QK_DIM
    half = DA_ROT_DIM // 2
    s = jnp.sin(ang)
    c_ref[...] = jnp.cos(ang)
    sa_ref[...] = jnp.where(sub < half, -s, 0.0)
    sb_ref[...] = jnp.where((sub >= half) & (sub < 2 * half), s, 0.0)


def _rope_tables(positions):
    T = positions.size
    tm = ROPE_TM
    half = DA_ROT_DIM // 2
    inv = 1.0 / (ROPE_THETA ** (jnp.arange(0, DA_ROT_DIM, 2, dtype=F32) / DA_ROT_DIM))
    sub = jnp.arange(LANES) % DA_QK_DIM
    inv_lane = jnp.where(sub < 2 * half, inv[sub % half], 0.0).reshape(1, LANES)
    tab = jax.ShapeDtypeStruct((T, LANES), F32)
    return pl.pallas_call(
        _rope_kernel,
        grid=(T // tm,),
        in_specs=[pl.BlockSpec((tm, 1), lambda i: (i, 0)),
                  pl.BlockSpec((1, LANES), lambda i: (0, 0))],
        out_specs=[pl.BlockSpec((tm, LANES), lambda i: (i, 0))] * 3,
        out_shape=[tab] * 3,
        compiler_params=_cparams(("parallel",)),
        name="rope_tables",
    )(positions.reshape(T, 1), inv_lane)


def _rotate(x, c, sa, sb):
    half = DA_ROT_DIM // 2
    return x * c + pltpu.roll(x, LANES - half, 1) * sa + pltpu.roll(x, half, 1) * sb


def _wprep_kernel(a_ref, o_ref):
    n_layers, tn, _ = o_ref.shape
    sub = lax.broadcasted_iota(jnp.int32, (SUBLANES, LANES), 0)
    low = {d: (sub & d) == 0 for d in (4, 2, 1)}

    def transpose8(a):
        for d in (4, 2, 1):
            nxt = list(a)
            for i in range(SUBLANES):
                if not i & d:
                    x, y = a[i], a[i + d]
                    nxt[i] = jnp.where(low[d], x, pltpu.roll(y, d, 0))
                    nxt[i + d] = jnp.where(low[d], pltpu.roll(x, SUBLANES - d, 0), y)
            a = nxt
        return a

    rows_per_iter = 2 * SUBLANES

    def body(it, carry):
        n0 = pl.multiple_of(it * rows_per_iter, rows_per_iter)
        for jb in range(a_ref.shape[1] // SUBLANES):
            js = pl.ds(jb * SUBLANES, SUBLANES)
            halves = [transpose8([a_ref[n0 + half * SUBLANES + i, js, :] for i in range(SUBLANES)])
                      for half in range(2)]
            for s in range(SUBLANES):
                kb, l = divmod(jb * SUBLANES + s, n_layers)
                tile = jnp.concatenate([halves[0][s], halves[1][s]], axis=0)
                o_ref[l, pl.ds(n0, rows_per_iter), kb * LANES:(kb + 1) * LANES] = tile.astype(o_ref.dtype)
        return carry

    lax.fori_loop(0, tn // rows_per_iter, body, 0)


def _native_rows(w_in):
    L, D, N = w_in.shape
    v = jnp.transpose(w_in.reshape(L, D // LANES, LANES, N), (3, 1, 0, 2))
    return v.reshape(N, (D // LANES) * L, LANES)


def _prep_rows(w_rows, L, D, tn, n, offset):
    return pl.pallas_call(
        _wprep_kernel,
        grid=(n,),
        in_specs=[pl.BlockSpec((pl.Element(tn), pl.Element(w_rows.shape[1]), pl.Element(LANES)),
                               lambda j: (offset(j), 0, 0))],
        out_specs=pl.BlockSpec((L, tn, D), lambda j: (0, j, 0)),
        out_shape=jax.ShapeDtypeStruct((L, n * tn, D), BF16),
        compiler_params=_cparams(("parallel",), PREP_VMEM_MB),
        name="w_in_prep",
    )(w_rows)


def _prep_w_in(w_in):
    L, D, _ = w_in.shape
    f0 = N_ALIGNED * SLOT
    d0 = f0 + FA_HEADS
    g0 = d0 + SLOT

    def slot_row(j):
        return jnp.where(j < N_GATE_SLOTS, g0 + SLOT * j, jnp.where(j == J_DH, d0, SLOT * (j - N_GATE_SLOTS)))

    w_rows = _native_rows(w_in)
    return (_prep_rows(w_rows, L, D, SLOT, N_SLOTS, slot_row),
            _prep_rows(w_rows, L, D, LANES, 1, lambda j: f0 + 0 * j))


def _dot_nt(a, b):
    return lax.dot_general(a, b, (((1,), (1,)), ((), ())), preferred_element_type=F32)


def _inproj_kernel(*refs, n_cast, layer):
    h_hbm, g_ref, w_hbm, wf_ref = refs[:4]
    cast_in = refs[4:4 + n_cast]
    out_ref, f_ref = refs[4 + n_cast:6 + n_cast]
    cast_out = refs[6 + n_cast:6 + 2 * n_cast]
    xn_ref, h_buf, h_sem, w_buf, w_sem = refs[-5:]
    n_slots, tn, _ = w_buf.shape
    nj = pl.num_programs(1)
    step = pl.program_id(0) * nj + pl.program_id(1)
    n_steps = pl.num_programs(0) * nj

    def w_copy(s):
        rows = pl.ds(pl.multiple_of((s % nj) * tn, LANES), tn)
        slot = s % n_slots
        return pltpu.make_async_copy(w_hbm.at[layer, rows, :], w_buf.at[slot], w_sem.at[slot])

    @pl.when(step == 0)
    def _():
        for s in range(n_slots - 1):
            w_copy(s).start()

    @pl.when(pl.program_id(1) == 0)
    def _():
        slot = _next_row_tile(h_hbm, h_buf, h_sem)
        xn_ref[...] = _rms(h_buf[slot], g_ref[...]).astype(BF16)
        f_ref[...] = _dot_nt(xn_ref[...], wf_ref[...])

    w_copy(step).wait()

    @pl.when(step + n_slots - 1 < n_steps)
    def _():
        w_copy(step + n_slots - 1).start()

    r = _dot_nt(xn_ref[...], w_buf[step % n_slots])
    for c in range(out_ref.shape[0]):
        out_ref[c] = r[:, c * LANES:(c + 1) * LANES].astype(out_ref.dtype)
    for src, dst in zip(cast_in, cast_out):
        dst[...] = src[...].astype(dst.dtype)


def _inproj(h, g, w_t, wf_t, l, to_cast, tm=INPROJ_TM, tn=INPROJ_TN):
    T, D = h.shape
    N = w_t.shape[1]
    ni, nj = T // tm, N // tn
    n_blk = CAST_BLOCKS
    assert ni * nj >= n_blk
    row = lambda i, j: (i, 0)
    blk = lambda i, j: jnp.minimum(i * nj + j, n_blk - 1)
    cast_in, cast_out, cast_shape = [], [], []
    for w in to_cast:
        _, R, C = w.shape
        cast_in.append(pl.BlockSpec((None, R // n_blk, C), lambda i, j: (l, blk(i, j), 0)))
        cast_out.append(pl.BlockSpec((R // n_blk, C), lambda i, j: (blk(i, j), 0)))
        cast_shape.append(jax.ShapeDtypeStruct((R, C), BF16))
    res = pl.pallas_call(
        functools.partial(_inproj_kernel, n_cast=len(to_cast), layer=l),
        grid=(ni, nj),
        in_specs=[pl.BlockSpec(memory_space=pl.ANY),
                  pl.BlockSpec((1, D), lambda i, j: (0, 0)),
                  pl.BlockSpec(memory_space=pl.ANY),
                  pl.BlockSpec((None, LANES, D), lambda i, j: (l, 0, 0))] + cast_in,
        out_specs=[pl.BlockSpec((tn // LANES, tm, LANES), lambda i, j: (j, i, 0)),
                   pl.BlockSpec((tm, LANES), row)] + cast_out,
        out_shape=[jax.ShapeDtypeStruct((N // LANES, T, LANES), BF16),
                   jax.ShapeDtypeStruct((T, LANES), F32)] + cast_shape,
        scratch_shapes=[pltpu.VMEM((tm, D), BF16), pltpu.VMEM((2, tm, D), F32), pltpu.SemaphoreType.DMA((2,)),
                        pltpu.VMEM((INPROJ_W_SLOTS, tn, D), BF16), pltpu.SemaphoreType.DMA((INPROJ_W_SLOTS,))],
        compiler_params=_cparams(("arbitrary", "arbitrary"), INPROJ_VMEM_MB),
        name="inproj",
    )(h, g, w_t, wf_t, *to_cast)
    return res[0], res[1], res[2:]


def _gmlp_kernel(u_ref, v_ref, lng_ref, lnb_ref, ws_ref, bst_ref, o_ref):
    v = jnp.concatenate([v_ref[g] for g in range(GM_GROUPS)], axis=1).astype(F32)
    mu = jnp.mean(v, axis=-1, keepdims=True)
    vc = v - mu
    var = jnp.mean(vc * vc, axis=-1, keepdims=True)
    vn = (vc * lax.rsqrt(var + NORM_EPS) * lng_ref[...] + lnb_ref[...]).astype(BF16)
    r = lax.broadcasted_iota(jnp.int32, (GM_CHUNK, GM_CHUNK), 0)
    c = lax.broadcasted_iota(jnp.int32, (GM_CHUNK, GM_CHUNK), 1)
    causal = r >= c
    for g in range(GM_GROUPS):
        wg = jnp.where(causal, ws_ref[g], 0.0).astype(BF16)
        bcol = bst_ref[:, g:g + 1]
        cols = slice(g * HEAD_COLS, (g + 1) * HEAD_COLS)
        for n in range(v.shape[0] // GM_CHUNK):
            rows = slice(n * GM_CHUNK, (n + 1) * GM_CHUNK)
            mixed = jnp.dot(wg, vn[rows, cols], preferred_element_type=F32) + bcol
            o_ref[g, rows, :] = (u_ref[g, rows, :].astype(F32) * mixed).astype(o_ref.dtype)


def _gmlp(proj, ln_g, ln_b, w_s, b_s_t, tm=GMLP_TM):
    T = proj.shape[1]
    const2 = lambda i: (0, 0)
    grp = (SLOT_GROUPS, tm, HEAD_COLS)
    return pl.pallas_call(
        _gmlp_kernel,
        grid=(T // tm,),
        in_specs=[pl.BlockSpec(grp, lambda i: (J_U, i, 0)),
                  pl.BlockSpec(grp, lambda i: (J_V, i, 0)),
                  pl.BlockSpec((1, SLOT), const2),
                  pl.BlockSpec((1, SLOT), const2),
                  pl.BlockSpec((GM_GROUPS, GM_CHUNK, GM_CHUNK), lambda i: (0, 0, 0)),
                  pl.BlockSpec((GM_CHUNK, GM_GROUPS), const2)],
        out_specs=pl.BlockSpec(grp, lambda i: (0, i, 0)),
        out_shape=jax.ShapeDtypeStruct((SLOT_GROUPS, T, HEAD_COLS), BF16),
        compiler_params=_cparams(("parallel",)),
        name="gmlp",
    )(proj, proj, ln_g, ln_b, w_s, b_s_t)


def _qk(q, k):
    return lax.dot_general(q, k, (((1,), (1,)), ((), ())), preferred_element_type=F32)


def _causal(rows, tq):
    r = lax.broadcasted_iota(jnp.int32, (rows, tq), 0) % tq
    c = lax.broadcasted_iota(jnp.int32, (rows, tq), 1)
    return r >= c


def _long_short_order(n):
    lo, hi, out = 0, n - 1, []
    while lo <= hi:
        out.append(hi)
        hi -= 1
        if lo <= hi:
            out.append(lo)
            lo += 1
    return out


def _pipelined_blocks(order, logits, weights, finish):
    assert len(order) >= 2
    s_q = [logits(i) for i in order[:2]]
    w_q = [weights(order[0], s_q.pop(0))]
    for n, i in enumerate(order):
        if n + 2 < len(order):
            s_q.append(logits(order[n + 2]))
        if n + 1 < len(order):
            w_q.append(weights(order[n + 1], s_q.pop(0)))
        finish(i, *w_q.pop(0))


def _mask_last_block(s, causal):
    tq = causal.shape[1]
    diag = jnp.where(causal, s[:, -tq:], MASK_VALUE)
    return diag if s.shape[1] == tq else jnp.concatenate([s[:, :-tq], diag], axis=1)


def _row_softmax(s, row_const=None):
    m = jnp.max(s, axis=-1, keepdims=True)
    if row_const is not None:
        m = (m + row_const) - row_const
    p = jnp.exp2(s - m)
    return p, jnp.sum(p, axis=-1, keepdims=True)


def _diff_attn_kernel(q_ref, k_ref, v_ref, c_ref, sa_ref, sb_ref, lam_ref, g_ref, o_ref,
                      qs_ref, kr_ref, *, tq, lambda_init):
    n_heads, S, _ = q_ref.shape
    c, sa, sb = c_ref[...], sa_ref[...], sb_ref[...]
    lp = lam_ref[...]

    def total(x):
        return jnp.sum(jnp.sum(x, axis=1, keepdims=True), axis=0, keepdims=True)

    lam = jnp.exp(total(lp[0:1] * lp[1:2])) - jnp.exp(total(lp[2:3] * lp[3:4])) + lambda_init
    causal = _causal(2 * tq, tq)
    g = g_ref[...]
    lane = lax.broadcasted_iota(jnp.int32, (S, HEAD_COLS), 1)

    for hh in range(n_heads):
        qf = _rotate(q_ref[hh].astype(F32), c, sa, sb) * (DA_QK_DIM ** -0.5 * LOG2E)
        qs_ref[hh, 0] = jnp.where(lane < DA_QK_DIM, qf, 0.0).astype(BF16)
        qs_ref[hh, 1] = jnp.where(lane >= DA_QK_DIM, qf, 0.0).astype(BF16)
        kr_ref[hh] = _rotate(k_ref[hh].astype(F32), c, sa, sb).astype(BF16)

        def logits(i, hh=hh):
            rows = pl.ds(i * tq, tq)
            q2 = jnp.concatenate([qs_ref[hh, 0, rows, :], qs_ref[hh, 1, rows, :]], axis=0)
            return _mask_last_block(_qk(q2, kr_ref[hh, pl.ds(0, (i + 1) * tq), :]), causal)

        def weights(i, s):
            p, l = _row_softmax(s)
            ratio = lam * l[:tq] / l[tq:]
            return (p[:tq] - ratio * p[tq:]).astype(BF16), l[:tq]

        def finish(i, w, l1, hh=hh):
            o = jnp.dot(w, v_ref[hh, pl.ds(0, (i + 1) * tq), :], preferred_element_type=F32) / l1
            o_ref[hh, pl.ds(i * tq, tq), :] = (_rms(o, g) * (1.0 - lambda_init)).astype(o_ref.dtype)

        _pipelined_blocks(_long_short_order(S // tq), logits, weights, finish)


def _diff_attn(proj, tables, lam, subln_g, B, S, lambda_init, tq=ATTN_TQ):
    T = proj.shape[1]
    hps = ATTN_HEADS_PER_STEP
    seq = lambda b, h: (b, 0)
    heads = (hps, S, HEAD_COLS)
    first = lambda slot: slot * SLOT_GROUPS // hps
    kern = functools.partial(_diff_attn_kernel, tq=tq, lambda_init=lambda_init)
    return pl.pallas_call(
        kern,
        grid=(B, DA_HEADS // hps),
        in_specs=[pl.BlockSpec(heads, lambda b, h: (first(J_BQ) + h, b, 0)),
                  pl.BlockSpec(heads, lambda b, h: (first(J_BK) + h, b, 0)),
                  pl.BlockSpec(heads, lambda b, h: (first(J_BV) + h, b, 0)),
                  pl.BlockSpec((S, LANES), seq),
                  pl.BlockSpec((S, LANES), seq),
                  pl.BlockSpec((S, LANES), seq),
                  pl.BlockSpec(lam.shape, lambda b, h: (0, 0)),
                  pl.BlockSpec((1, HEAD_COLS), lambda b, h: (0, 0))],
        out_specs=pl.BlockSpec(heads, lambda b, h: (h, b, 0)),
        out_shape=jax.ShapeDtypeStruct((DA_HEADS, T, HEAD_COLS), BF16),
        scratch_shapes=[pltpu.VMEM((hps, 2, S, HEAD_COLS), BF16), pltpu.VMEM((hps, S, HEAD_COLS), BF16)],
        compiler_params=_cparams(("parallel", "parallel")),
        name="diff_attn",
    )(proj, proj, proj, *tables, lam, subln_g)


def _fcum_kernel(f_ref, bf_ref, cum_ref, cumt_ref, *, blk):
    x = f_ref[...] + bf_ref[...]
    logf = jnp.minimum(x, 0.0) - jnp.log1p(jnp.exp(-jnp.abs(x)))
    r = lax.broadcasted_iota(jnp.int32, (blk, blk), 0)
    c = lax.broadcasted_iota(jnp.int32, (blk, blk), 1)
    tri = (r >= c).astype(F32)
    carry = jnp.zeros((1, LANES), F32)
    for n in range(x.shape[0] // blk):
        part = jnp.dot(tri, logf[n * blk:(n + 1) * blk], preferred_element_type=F32,
                       precision=lax.Precision.HIGHEST) + carry
        cum_ref[n * blk:(n + 1) * blk, :] = part
        carry = part[blk - 1:blk, :]
    cum_t = cum_ref[...].T
    for hh in range(FA_HEADS):
        cumt_ref[hh] = cum_t[hh:hh + 1, :]


def _fcum(f_logit, b_f, B, S):
    T = f_logit.shape[0]
    return pl.pallas_call(
        functools.partial(_fcum_kernel, blk=FCUM_BLOCK),
        grid=(B,),
        in_specs=[pl.BlockSpec((S, LANES), lambda b: (b, 0)),
                  pl.BlockSpec((1, LANES), lambda b: (0, 0))],
        out_specs=[pl.BlockSpec((S, LANES), lambda b: (b, 0)),
                   pl.BlockSpec((FA_HEADS, 1, S), lambda b: (b, 0, 0))],
        out_shape=[jax.ShapeDtypeStruct((T, LANES), F32),
                   jax.ShapeDtypeStruct((B * FA_HEADS, 1, S), F32)],
        compiler_params=_cparams(("parallel",)),
        name="forget_cumsum",
    )(f_logit, b_f)


def _fox_attn_kernel(q_ref, k_ref, v_ref, cum_ref, cumt_ref, o_ref, qs_ref, *, tq):
    n_heads, S, _ = q_ref.shape
    causal = _causal(tq, tq)
    for hh in range(n_heads):
        head = pl.program_id(1) * n_heads + hh
        qs_ref[hh] = (q_ref[hh].astype(F32) * (FA_HEAD_DIM ** -0.5 * LOG2E)).astype(BF16)

        def logits(i, hh=hh):
            keys = pl.ds(0, (i + 1) * tq)
            s = _qk(qs_ref[hh, pl.ds(i * tq, tq), :], k_ref[hh, keys, :]) - cumt_ref[hh, :, keys] * LOG2E
            return _mask_last_block(s, causal)

        def weights(i, s, head=head):
            cum = cum_ref[pl.ds(i * tq, tq), :]
            lane = lax.broadcasted_iota(jnp.int32, cum.shape, 1)
            cq = jnp.sum(jnp.where(lane == head, cum, 0.0), axis=-1, keepdims=True) * LOG2E
            p, l = _row_softmax(s, cq)
            return p.astype(BF16), l

        def finish(i, p, l, hh=hh):
            o = jnp.dot(p, v_ref[hh, pl.ds(0, (i + 1) * tq), :], preferred_element_type=F32)
            o_ref[hh, pl.ds(i * tq, tq), :] = (o / l).astype(o_ref.dtype)

        _pipelined_blocks(_long_short_order(S // tq), logits, weights, finish)


def _fox_attn(proj, cum, cum_t, B, S, tq=ATTN_TQ):
    T = proj.shape[1]
    hps = ATTN_HEADS_PER_STEP
    heads = (hps, S, HEAD_COLS)
    first = lambda slot: slot * SLOT_GROUPS // hps
    return pl.pallas_call(
        functools.partial(_fox_attn_kernel, tq=tq),
        grid=(B, FA_HEADS // hps),
        in_specs=[pl.BlockSpec(heads, lambda b, h: (first(J_CQ) + h, b, 0)),
                  pl.BlockSpec(heads, lambda b, h: (first(J_CK) + h, b, 0)),
                  pl.BlockSpec(heads, lambda b, h: (first(J_CV) + h, b, 0)),
                  pl.BlockSpec((S, LANES), lambda b, h: (b, 0)),
                  pl.BlockSpec((hps, 1, S), lambda b, h: (b * (FA_HEADS // hps) + h, 0, 0))],
        out_specs=pl.BlockSpec(heads, lambda b, h: (h, b, 0)),
        out_shape=jax.ShapeDtypeStruct((FA_HEADS, T, HEAD_COLS), BF16),
        scratch_shapes=[pltpu.VMEM((hps, S, HEAD_COLS), BF16)],
        compiler_params=_cparams(("parallel", "parallel")),
        name="fox_attn",
    )(proj, proj, proj, cum, cum_t)


def _pool_kernel(h_ref, w_ref, sc_ref, o_ref):
    S = h_ref.shape[1]
    t = lax.broadcasted_iota(jnp.int32, (S, HEAD_COLS), 0)
    for g, win in enumerate(POOL_WINDOWS):
        cols = slice(g * HEAD_COLS, (g + 1) * HEAD_COLS)
        x = h_ref[g].astype(F32)
        tot = x
        span = 1
        while span < win:
            tot = tot + jnp.where(t >= span, pltpu.roll(tot, span, 0), 0.0)
            span *= 2
        cnt = jnp.minimum(t + 1, win).astype(F32)
        pooled = (tot / cnt - x).astype(BF16)
        y = jnp.dot(pooled, w_ref[g].astype(BF16), preferred_element_type=F32)
        o_ref[g] = (y * sc_ref[:, cols]).astype(o_ref.dtype)


def _pool(proj, w_pool, scale, B, S):
    T = proj.shape[1]
    grp = (SLOT_GROUPS, S, HEAD_COLS)
    return pl.pallas_call(
        _pool_kernel,
        grid=(B,),
        in_specs=[pl.BlockSpec(grp, lambda b: (J_DH, b, 0)),
                  pl.BlockSpec(w_pool.shape, lambda b: (0, 0, 0)),
                  pl.BlockSpec((1, SLOT), lambda b: (0, 0))],
        out_specs=pl.BlockSpec(grp, lambda b: (0, b, 0)),
        out_shape=jax.ShapeDtypeStruct((SLOT_GROUPS, T, HEAD_COLS), BF16),
        compiler_params=_cparams(("parallel",)),
        name="pool",
    )(proj, w_pool, scale)


def _merge_kernel(oa_ref, ob_ref, oc_ref, od_ref, gate_ref, wb_ref, wo_ref, g_ref, h_ref, out_ref):
    def wide(ref, g0, n):
        return jnp.concatenate([ref[g] for g in range(g0, g0 + n)], axis=1)

    per_branch = h_ref.shape[1] // HEAD_COLS
    merged = None
    for n, br in enumerate((oa_ref, ob_ref, oc_ref, od_ref)):
        bd = jnp.dot(wide(br, 0, SLOT_GROUPS), wb_ref[n], preferred_element_type=F32)
        term = jax.nn.sigmoid(wide(gate_ref, n * per_branch, per_branch).astype(F32)) * bd
        merged = term if merged is None else merged + term
    y = jnp.dot(merged.astype(BF16), wo_ref[...], preferred_element_type=F32)
    out_ref[...] = h_ref[...] + _rms(y, g_ref[...])


def _merge(branches, gates, w_branch, w_out, g, h, tm=MERGE_TM):
    T, D = h.shape
    row = lambda i: (i, 0)
    once = pl.Buffered(1)
    return pl.pallas_call(
        _merge_kernel,
        grid=(T // tm,),
        in_specs=[pl.BlockSpec((SLOT_GROUPS, tm, HEAD_COLS), lambda i: (0, i, 0))] * N_BRANCH + [
            pl.BlockSpec((N_BRANCH * D // HEAD_COLS, tm, HEAD_COLS), lambda i: (0, i, 0)),
            pl.BlockSpec(w_branch.shape, lambda i: (0, 0, 0), pipeline_mode=once),
            pl.BlockSpec(w_out.shape, lambda i: (0, 0), pipeline_mode=once),
            pl.BlockSpec((1, D), lambda i: (0, 0)),
            pl.BlockSpec((tm, D), row)],
        out_specs=pl.BlockSpec((tm, D), row),
        out_shape=jax.ShapeDtypeStruct((T, D), F32),
        compiler_params=_cparams(("parallel",), MERGE_VMEM_MB),
        name="merge_out",
    )(*branches, gates, w_branch, w_out, g, h)


def _ffn_kernel(h_hbm, g1_ref, wu_ref, wd_ref, g2_ref, out_ref, hn_ref, acc_ref, h_buf, h_sem):
    f = pl.program_id(1)

    @pl.when(f == 0)
    def _():
        slot = _next_row_tile(h_hbm, h_buf, h_sem)
        hn_ref[...] = _rms(h_buf[slot], g1_ref[...]).astype(BF16)
        acc_ref[...] = jnp.zeros_like(acc_ref)

    up = jnp.dot(hn_ref[...], wu_ref[...], preferred_element_type=F32)
    a = jnp.square(jnp.maximum(up, 0.0)).astype(BF16)
    acc_ref[...] += jnp.dot(a, wd_ref[...], preferred_element_type=F32)

    @pl.when(f == pl.num_programs(1) - 1)
    def _():
        g2 = g2_ref[...]
        slot = pl.program_id(0) % 2
        for r in _row_chunks(out_ref.shape[0]):
            out_ref[r, :] = h_buf[slot, r, :] + _rms(acc_ref[r, :], g2)


def _ffn(h, g1, w_up, w_down, g2, tm=FFN_TM, tf=FFN_TF):
    T, D = h.shape
    F = w_up.shape[1]
    row = lambda i, f: (i, 0)
    vec = lambda i, f: (0, 0)
    return pl.pallas_call(
        _ffn_kernel,
        grid=(T // tm, F // tf),
        in_specs=[pl.BlockSpec(memory_space=pl.ANY),
                  pl.BlockSpec((1, D), vec),
                  pl.BlockSpec((D, tf), lambda i, f: (0, f)),
                  pl.BlockSpec((tf, D), lambda i, f: (f, 0)),
                  pl.BlockSpec((1, D), vec)],
        out_specs=pl.BlockSpec((tm, D), row),
        out_shape=jax.ShapeDtypeStruct((T, D), F32),
        scratch_shapes=[pltpu.VMEM((tm, D), BF16), pltpu.VMEM((tm, D), F32), pltpu.VMEM((2, tm, D), F32),
                        pltpu.SemaphoreType.DMA((2,))],
        compiler_params=_cparams(("arbitrary", "arbitrary"), FFN_VMEM_MB),
        name="ffn",
    )(h, g1, w_up, w_down, g2)


def kernel(x, positions, norm_mix_pre, norm_mix_post, norm_ffn_pre, norm_ffn_post, w_in, gm_ln_g,
           gm_ln_b, gm_w_s, gm_b_s, da_lambda, da_subln_g, fa_b_f, pool_w, pool_scale, w_branch,
           w_out, w_ffn_up, w_ffn_down):
    B, S, D = x.shape
    T = B * S
    h = x.reshape(T, D)
    tables = _rope_tables(positions)
    w_t, wf_t = _prep_w_in(w_in)
    L, NB, W, _ = w_branch.shape
    to_cast = (w_branch.reshape(L, NB * W, D), w_out, w_ffn_up, w_ffn_down)
    row = lambda a: a.reshape(1, -1)
    for l in range(DEPTH):
        lambda_init = 0.8 - 0.6 * math.exp(-0.3 * l)
        proj, f_logit, (wb, wo, wu, wd) = _inproj(h, row(norm_mix_pre[l]), w_t, wf_t, l, to_cast)

        o_a = _gmlp(proj, row(gm_ln_g[l]), row(gm_ln_b[l]), gm_w_s[l], gm_b_s[l].T)
        o_b = _diff_attn(proj, tables, da_lambda[l], row(da_subln_g[l]), B, S, lambda_init)
        b_f = jnp.pad(fa_b_f[l], (0, LANES - FA_HEADS)).reshape(1, LANES)
        cum, cum_t = _fcum(f_logit, b_f, B, S)
        o_c = _fox_attn(proj, cum, cum_t, B, S)
        o_d = _pool(proj, pool_w[l], row(pool_scale[l]), B, S)

        h = _merge((o_a, o_b, o_c, o_d), proj, wb.reshape(NB, W, D), wo, row(norm_mix_post[l]), h)
        h = _ffn(h, row(norm_ffn_pre[l]), wu, wd, row(norm_ffn_post[l]))
    return h.reshape(B, S, D)
```

```python
import functools
import math

import jax
import jax.numpy as jnp
from jax import lax
from jax.experimental import pallas as pl
from jax.experimental.pallas import tpu as pltpu

F32 = jnp.float32
BF16 = jnp.bfloat16

DEPTH = 2
N_BRANCH = 4
GM_GROUPS = 4
GM_CHUNK = 128
DA_HEADS = 4
DA_QK_DIM = 64
DA_ROT_DIM = 16
FA_HEADS = 4
FA_HEAD_DIM = 128
POOL_WINDOWS = (2, 4, 8, 16)
ROPE_THETA = 500000.0
NORM_EPS = 1e-6
MASK_VALUE = -1e30
LOG2E = math.log2(math.e)

LANES = 128
SUBLANES = 8
ROW_CHUNK = 2 * SUBLANES
CAST_BLOCKS = 64
HEAD_COLS = 128
SLOT = 512
SLOT_GROUPS = SLOT // HEAD_COLS
ATTN_HEADS_PER_STEP = 2

ROPE_TM = 1024
PREP_VMEM_MB = 40
INPROJ_TM, INPROJ_TN, INPROJ_VMEM_MB = 1024, 1280, 56
GMLP_TM = 2048
ATTN_TQ = 256
FCUM_BLOCK = 256
MERGE_TM, MERGE_VMEM_MB = 256, 48
FFN_TM, FFN_TF, FFN_VMEM_MB = 512, 1024, 48


N_GATE_SLOTS = 16
J_U, J_V, J_BQ, J_BK, J_BV, J_CQ, J_CK, J_CV, J_DH = range(N_GATE_SLOTS, N_GATE_SLOTS + 9)
N_SLOTS = J_DH + 1
N_ALIGNED = 8


def _rms(x, g):
    return x * lax.rsqrt(jnp.mean(x * x, axis=-1, keepdims=True) + NORM_EPS) * g


def _row_chunks(n_rows, rows=ROW_CHUNK):
    return [pl.ds(r0, rows) for r0 in range(0, n_rows, rows)]


def _next_row_tile(x_hbm, buf, sem):
    i, n = pl.program_id(0), pl.num_programs(0)
    tm = buf.shape[1]

    def copy(tile, slot):
        rows = pl.ds(pl.multiple_of(tile * tm, tm), tm)
        return pltpu.make_async_copy(x_hbm.at[rows, :], buf.at[slot], sem.at[slot])

    slot = i % 2

    @pl.when(i == 0)
    def _():
        copy(0, 0).start()

    copy(i, slot).wait()

    @pl.when(i + 1 < n)
    def _():
        copy(i + 1, 1 - slot).start()

    return slot


def _cparams(sem, vmem_mb=None):
    kw = dict(dimension_semantics=sem)
    if vmem_mb is not None:
        kw["vmem_limit_bytes"] = vmem_mb * 1024 * 1024
    return pltpu.CompilerParams(**kw)


def _rope_kernel(pos_ref, inv_ref, c_ref, sa_ref, sb_ref):
    ang = pos_ref[...].astype(F32) * inv_ref[...]
    sub = lax.broadcasted_iota(jnp.int32, ang.shape, 1) % DA_QK_DIM
    half = DA_ROT_DIM // 2
    s = jnp.sin(ang)
    c_ref[...] = jnp.cos(ang)
    sa_ref[...] = jnp.where(sub < half, -s, 0.0)
    sb_ref[...] = jnp.where((sub >= half) & (sub < 2 * half), s, 0.0)


def _rope_tables(positions):
    T = positions.size
    tm = ROPE_TM
    half = DA_ROT_DIM // 2
    inv = 1.0 / (ROPE_THETA ** (jnp.arange(0, DA_ROT_DIM, 2, dtype=F32) / DA_ROT_DIM))
    sub = jnp.arange(LANES) % DA_QK_DIM
    inv_lane = jnp.where(sub < 2 * half, inv[sub % half], 0.0).reshape(1, LANES)
    tab = jax.ShapeDtypeStruct((T, LANES), F32)
    return pl.pallas_call(
        _rope_kernel,
        grid=(T // tm,),
        in_specs=[pl.BlockSpec((tm, 1), lambda i: (i, 0)),
                  pl.BlockSpec((1, LANES), lambda i: (0, 0))],
        out_specs=[pl.BlockSpec((tm, LANES), lambda i: (i, 0))] * 3,
        out_shape=[tab] * 3,
        compiler_params=_cparams(("parallel",)),
        name="rope_tables",
    )(positions.reshape(T, 1), inv_lane)


def _rotate(x, c, sa, sb):
    half = DA_ROT_DIM // 2
    return x * c + pltpu.roll(x, LANES - half, 1) * sa + pltpu.roll(x, half, 1) * sb


def _wprep_kernel(a_ref, o_ref):
    n_layers, tn, _ = o_ref.shape
    sub = lax.broadcasted_iota(jnp.int32, (SUBLANES, LANES), 0)
    low = {d: (sub & d) == 0 for d in (4, 2, 1)}

    def transpose8(a):
        for d in (4, 2, 1):
            nxt = list(a)
            for i in range(SUBLANES):
                if not i & d:
                    x, y = a[i], a[i + d]
                    nxt[i] = jnp.where(low[d], x, pltpu.roll(y, d, 0))
                    nxt[i + d] = jnp.where(low[d], pltpu.roll(x, SUBLANES - d, 0), y)
            a = nxt
        return a

    rows_per_iter = 2 * SUBLANES

    def body(it, carry):
        n0 = pl.multiple_of(it * rows_per_iter, rows_per_iter)
        for jb in range(a_ref.shape[1] // SUBLANES):
            js = pl.ds(jb * SUBLANES, SUBLANES)
            halves = [transpose8([a_ref[n0 + half * SUBLANES + i, js, :] for i in range(SUBLANES)])
                      for half in range(2)]
            for s in range(SUBLANES):
                kb, l = divmod(jb * SUBLANES + s, n_layers)
                tile = jnp.concatenate([halves[0][s], halves[1][s]], axis=0)
                o_ref[l, pl.ds(n0, rows_per_iter), kb * LANES:(kb + 1) * LANES] = tile.astype(o_ref.dtype)
        return carry

    lax.fori_loop(0, tn // rows_per_iter, body, 0)


def _native_rows(w_in):
    L, D, N = w_in.shape
    v = jnp.transpose(w_in.reshape(L, D // LANES, LANES, N), (3, 1, 0, 2))
    return v.reshape(N, (D // LANES) * L, LANES)


def _prep_rows(w_rows, L, D, tn, n, offset):
    return pl.pallas_call(
        _wprep_kernel,
        grid=(n,),
        in_specs=[pl.BlockSpec((pl.Element(tn), pl.Element(w_rows.shape[1]), pl.Element(LANES)),
                               lambda j: (offset(j), 0, 0))],
        out_specs=pl.BlockSpec((L, tn, D), lambda j: (0, j, 0)),
        out_shape=jax.ShapeDtypeStruct((L, n * tn, D), BF16),
        compiler_params=_cparams(("parallel",), PREP_VMEM_MB),
        name="w_in_prep",
    )(w_rows)


def _prep_w_in(w_in):
    L, D, _ = w_in.shape
    f0 = N_ALIGNED * SLOT
    d0 = f0 + FA_HEADS
    g0 = d0 + SLOT

    def slot_row(j):
        return jnp.where(j < N_GATE_SLOTS, g0 + SLOT * j, jnp.where(j == J_DH, d0, SLOT * (j - N_GATE_SLOTS)))

    w_rows = _native_rows(w_in)
    return (_prep_rows(w_rows, L, D, SLOT, N_SLOTS, slot_row),
            _prep_rows(w_rows, L, D, LANES, 1, lambda j: f0 + 0 * j))


def _dot_nt(a, b):
    return lax.dot_general(a, b, (((1,), (1,)), ((), ())), preferred_element_type=F32)


def _inproj_kernel(*refs, n_cast):
    h_hbm, g_ref, w_ref, wf_ref = refs[:4]
    cast_in = refs[4:4 + n_cast]
    out_ref, f_ref = refs[4 + n_cast:6 + n_cast]
    cast_out = refs[6 + n_cast:6 + 2 * n_cast]
    xn_ref, h_buf, h_sem = refs[-3:]

    @pl.when(pl.program_id(1) == 0)
    def _():
        slot = _next_row_tile(h_hbm, h_buf, h_sem)
        xn_ref[...] = _rms(h_buf[slot], g_ref[...]).astype(BF16)
        f_ref[...] = _dot_nt(xn_ref[...], wf_ref[...])

    r = _dot_nt(xn_ref[...], w_ref[...])
    for c in range(out_ref.shape[0]):
        out_ref[c] = r[:, c * LANES:(c + 1) * LANES].astype(out_ref.dtype)
    for src, dst in zip(cast_in, cast_out):
        dst[...] = src[...].astype(dst.dtype)


def _inproj(h, g, w_t, wf_t, l, to_cast, tm=INPROJ_TM, tn=INPROJ_TN):
    T, D = h.shape
    N = w_t.shape[1]
    ni, nj = T // tm, N // tn
    n_blk = CAST_BLOCKS
    assert ni * nj >= n_blk
    row = lambda i, j: (i, 0)
    blk = lambda i, j: jnp.minimum(i * nj + j, n_blk - 1)
    cast_in, cast_out, cast_shape = [], [], []
    for w in to_cast:
        _, R, C = w.shape
        cast_in.append(pl.BlockSpec((None, R // n_blk, C), lambda i, j: (l, blk(i, j), 0)))
        cast_out.append(pl.BlockSpec((R // n_blk, C), lambda i, j: (blk(i, j), 0)))
        cast_shape.append(jax.ShapeDtypeStruct((R, C), BF16))
    res = pl.pallas_call(
        functools.partial(_inproj_kernel, n_cast=len(to_cast)),
        grid=(ni, nj),
        in_specs=[pl.BlockSpec(memory_space=pl.ANY),
                  pl.BlockSpec((1, D), lambda i, j: (0, 0)),
                  pl.BlockSpec((None, tn, D), lambda i, j: (l, j, 0)),
                  pl.BlockSpec((None, LANES, D), lambda i, j: (l, 0, 0))] + cast_in,
        out_specs=[pl.BlockSpec((tn // LANES, tm, LANES), lambda i, j: (j, i, 0)),
                   pl.BlockSpec((tm, LANES), row)] + cast_out,
        out_shape=[jax.ShapeDtypeStruct((N // LANES, T, LANES), BF16),
                   jax.ShapeDtypeStruct((T, LANES), F32)] + cast_shape,
        scratch_shapes=[pltpu.VMEM((tm, D), BF16), pltpu.VMEM((2, tm, D), F32), pltpu.SemaphoreType.DMA((2,))],
        compiler_params=_cparams(("arbitrary", "arbitrary"), INPROJ_VMEM_MB),
        name="inproj",
    )(h, g, w_t, wf_t, *to_cast)
    return res[0], res[1], res[2:]


def _gmlp_kernel(u_ref, v_ref, lng_ref, lnb_ref, ws_ref, bst_ref, o_ref):
    v = jnp.concatenate([v_ref[g] for g in range(GM_GROUPS)], axis=1).astype(F32)
    mu = jnp.mean(v, axis=-1, keepdims=True)
    vc = v - mu
    var = jnp.mean(vc * vc, axis=-1, keepdims=True)
    vn = (vc * lax.rsqrt(var + NORM_EPS) * lng_ref[...] + lnb_ref[...]).astype(BF16)
    r = lax.broadcasted_iota(jnp.int32, (GM_CHUNK, GM_CHUNK), 0)
    c = lax.broadcasted_iota(jnp.int32, (GM_CHUNK, GM_CHUNK), 1)
    causal = r >= c
    for g in range(GM_GROUPS):
        wg = jnp.where(causal, ws_ref[g], 0.0).astype(BF16)
        bcol = bst_ref[:, g:g + 1]
        cols = slice(g * HEAD_COLS, (g + 1) * HEAD_COLS)
        for n in range(v.shape[0] // GM_CHUNK):
            rows = slice(n * GM_CHUNK, (n + 1) * GM_CHUNK)
            mixed = jnp.dot(wg, vn[rows, cols], preferred_element_type=F32) + bcol
            o_ref[g, rows, :] = (u_ref[g, rows, :].astype(F32) * mixed).astype(o_ref.dtype)


def _gmlp(proj, ln_g, ln_b, w_s, b_s_t, tm=GMLP_TM):
    T = proj.shape[1]
    const2 = lambda i: (0, 0)
    grp = (SLOT_GROUPS, tm, HEAD_COLS)
    return pl.pallas_call(
        _gmlp_kernel,
        grid=(T // tm,),
        in_specs=[pl.BlockSpec(grp, lambda i: (J_U, i, 0)),
                  pl.BlockSpec(grp, lambda i: (J_V, i, 0)),
                  pl.BlockSpec((1, SLOT), const2),
                  pl.BlockSpec((1, SLOT), const2),
                  pl.BlockSpec((GM_GROUPS, GM_CHUNK, GM_CHUNK), lambda i: (0, 0, 0)),
                  pl.BlockSpec((GM_CHUNK, GM_GROUPS), const2)],
        out_specs=pl.BlockSpec(grp, lambda i: (0, i, 0)),
        out_shape=jax.ShapeDtypeStruct((SLOT_GROUPS, T, HEAD_COLS), BF16),
        compiler_params=_cparams(("parallel",)),
        name="gmlp",
    )(proj, proj, ln_g, ln_b, w_s, b_s_t)


def _qk(q, k):
    return lax.dot_general(q, k, (((1,), (1,)), ((), ())), preferred_element_type=F32)


def _causal(rows, tq):
    r = lax.broadcasted_iota(jnp.int32, (rows, tq), 0) % tq
    c = lax.broadcasted_iota(jnp.int32, (rows, tq), 1)
    return r >= c


def _long_short_order(n):
    lo, hi, out = 0, n - 1, []
    while lo <= hi:
        out.append(hi)
        hi -= 1
        if lo <= hi:
            out.append(lo)
            lo += 1
    return out


def _pipelined_blocks(order, logits, weights, finish):
    assert len(order) >= 2
    s_q = [logits(i) for i in order[:2]]
    w_q = [weights(order[0], s_q.pop(0))]
    for n, i in enumerate(order):
        if n + 2 < len(order):
            s_q.append(logits(order[n + 2]))
        if n + 1 < len(order):
            w_q.append(weights(order[n + 1], s_q.pop(0)))
        finish(i, *w_q.pop(0))


def _mask_last_block(s, causal):
    tq = causal.shape[1]
    diag = jnp.where(causal, s[:, -tq:], MASK_VALUE)
    return diag if s.shape[1] == tq else jnp.concatenate([s[:, :-tq], diag], axis=1)


def _row_softmax(s, row_const=None):
    m = jnp.max(s, axis=-1, keepdims=True)
    if row_const is not None:
        m = (m + row_const) - row_const
    p = jnp.exp2(s - m)
    return p, jnp.sum(p, axis=-1, keepdims=True)


def _diff_attn_kernel(q_ref, k_ref, v_ref, c_ref, sa_ref, sb_ref, lam_ref, g_ref, o_ref,
                      qs_ref, kr_ref, *, tq, lambda_init):
    n_heads, S, _ = q_ref.shape
    c, sa, sb = c_ref[...], sa_ref[...], sb_ref[...]
    lp = lam_ref[...]

    def total(x):
        return jnp.sum(jnp.sum(x, axis=1, keepdims=True), axis=0, keepdims=True)

    lam = jnp.exp(total(lp[0:1] * lp[1:2])) - jnp.exp(total(lp[2:3] * lp[3:4])) + lambda_init
    causal = _causal(2 * tq, tq)
    g = g_ref[...]
    lane = lax.broadcasted_iota(jnp.int32, (S, HEAD_COLS), 1)

    for hh in range(n_heads):
        qf = _rotate(q_ref[hh].astype(F32), c, sa, sb) * (DA_QK_DIM ** -0.5 * LOG2E)
        qs_ref[hh, 0] = jnp.where(lane < DA_QK_DIM, qf, 0.0).astype(BF16)
        qs_ref[hh, 1] = jnp.where(lane >= DA_QK_DIM, qf, 0.0).astype(BF16)
        kr_ref[hh] = _rotate(k_ref[hh].astype(F32), c, sa, sb).astype(BF16)

        def logits(i, hh=hh):
            rows = pl.ds(i * tq, tq)
            q2 = jnp.concatenate([qs_ref[hh, 0, rows, :], qs_ref[hh, 1, rows, :]], axis=0)
            return _mask_last_block(_qk(q2, kr_ref[hh, pl.ds(0, (i + 1) * tq), :]), causal)

        def weights(i, s):
            p, l = _row_softmax(s)
            ratio = lam * l[:tq] / l[tq:]
            return (p[:tq] - ratio * p[tq:]).astype(BF16), l[:tq]

        def finish(i, w, l1, hh=hh):
            o = jnp.dot(w, v_ref[hh, pl.ds(0, (i + 1) * tq), :], preferred_element_type=F32) / l1
            o_ref[hh, pl.ds(i * tq, tq), :] = (_rms(o, g) * (1.0 - lambda_init)).astype(o_ref.dtype)

        _pipelined_blocks(_long_short_order(S // tq), logits, weights, finish)


def _diff_attn(proj, tables, lam, subln_g, B, S, lambda_init, tq=ATTN_TQ):
    T = proj.shape[1]
    hps = ATTN_HEADS_PER_STEP
    seq = lambda b, h: (b, 0)
    heads = (hps, S, HEAD_COLS)
    first = lambda slot: slot * SLOT_GROUPS // hps
    kern = functools.partial(_diff_attn_kernel, tq=tq, lambda_init=lambda_init)
    return pl.pallas_call(
        kern,
        grid=(B, DA_HEADS // hps),
        in_specs=[pl.BlockSpec(heads, lambda b, h: (first(J_BQ) + h, b, 0)),
                  pl.BlockSpec(heads, lambda b, h: (first(J_BK) + h, b, 0)),
                  pl.BlockSpec(heads, lambda b, h: (first(J_BV) + h, b, 0)),
                  pl.BlockSpec((S, LANES), seq),
                  pl.BlockSpec((S, LANES), seq),
                  pl.BlockSpec((S, LANES), seq),
                  pl.BlockSpec(lam.shape, lambda b, h: (0, 0)),
                  pl.BlockSpec((1, HEAD_COLS), lambda b, h: (0, 0))],
        out_specs=pl.BlockSpec(heads, lambda b, h: (h, b, 0)),
        out_shape=jax.ShapeDtypeStruct((DA_HEADS, T, HEAD_COLS), BF16),
        scratch_shapes=[pltpu.VMEM((hps, 2, S, HEAD_COLS), BF16), pltpu.VMEM((hps, S, HEAD_COLS), BF16)],
        compiler_params=_cparams(("parallel", "parallel")),
        name="diff_attn",
    )(proj, proj, proj, *tables, lam, subln_g)


def _fcum_kernel(f_ref, bf_ref, cum_ref, cumt_ref, *, blk):
    x = f_ref[...] + bf_ref[...]
    logf = jnp.minimum(x, 0.0) - jnp.log1p(jnp.exp(-jnp.abs(x)))
    r = lax.broadcasted_iota(jnp.int32, (blk, blk), 0)
    c = lax.broadcasted_iota(jnp.int32, (blk, blk), 1)
    tri = (r >= c).astype(F32)
    carry = jnp.zeros((1, LANES), F32)
    for n in range(x.shape[0] // blk):
        part = jnp.dot(tri, logf[n * blk:(n + 1) * blk], preferred_element_type=F32,
                       precision=lax.Precision.HIGHEST) + carry
        cum_ref[n * blk:(n + 1) * blk, :] = part
        carry = part[blk - 1:blk, :]
    cum_t = cum_ref[...].T
    for hh in range(FA_HEADS):
        cumt_ref[hh] = cum_t[hh:hh + 1, :]


def _fcum(f_logit, b_f, B, S):
    T = f_logit.shape[0]
    return pl.pallas_call(
        functools.partial(_fcum_kernel, blk=FCUM_BLOCK),
        grid=(B,),
        in_specs=[pl.BlockSpec((S, LANES), lambda b: (b, 0)),
                  pl.BlockSpec((1, LANES), lambda b: (0, 0))],
        out_specs=[pl.BlockSpec((S, LANES), lambda b: (b, 0)),
                   pl.BlockSpec((FA_HEADS, 1, S), lambda b: (b, 0, 0))],
        out_shape=[jax.ShapeDtypeStruct((T, LANES), F32),
                   jax.ShapeDtypeStruct((B * FA_HEADS, 1, S), F32)],
        compiler_params=_cparams(("parallel",)),
        name="forget_cumsum",
    )(f_logit, b_f)


def _fox_attn_kernel(q_ref, k_ref, v_ref, cum_ref, cumt_ref, o_ref, qs_ref, *, tq):
    n_heads, S, _ = q_ref.shape
    causal = _causal(tq, tq)
    for hh in range(n_heads):
        head = pl.program_id(1) * n_heads + hh
        qs_ref[hh] = (q_ref[hh].astype(F32) * (FA_HEAD_DIM ** -0.5 * LOG2E)).astype(BF16)

        def logits(i, hh=hh):
            keys = pl.ds(0, (i + 1) * tq)
            s = _qk(qs_ref[hh, pl.ds(i * tq, tq), :], k_ref[hh, keys, :]) - cumt_ref[hh, :, keys] * LOG2E
            return _mask_last_block(s, causal)

        def weights(i, s, head=head):
            cum = cum_ref[pl.ds(i * tq, tq), :]
            lane = lax.broadcasted_iota(jnp.int32, cum.shape, 1)
            cq = jnp.sum(jnp.where(lane == head, cum, 0.0), axis=-1, keepdims=True) * LOG2E
            p, l = _row_softmax(s, cq)
            return p.astype(BF16), l

        def finish(i, p, l, hh=hh):
            o = jnp.dot(p, v_ref[hh, pl.ds(0, (i + 1) * tq), :], preferred_element_type=F32)
            o_ref[hh, pl.ds(i * tq, tq), :] = (o / l).astype(o_ref.dtype)

        _pipelined_blocks(_long_short_order(S // tq), logits, weights, finish)


def _fox_attn(proj, cum, cum_t, B, S, tq=ATTN_TQ):
    T = proj.shape[1]
    hps = ATTN_HEADS_PER_STEP
    heads = (hps, S, HEAD_COLS)
    first = lambda slot: slot * SLOT_GROUPS // hps
    return pl.pallas_call(
        functools.partial(_fox_attn_kernel, tq=tq),
        grid=(B, FA_HEADS // hps),
        in_specs=[pl.BlockSpec(heads, lambda b, h: (first(J_CQ) + h, b, 0)),
                  pl.BlockSpec(heads, lambda b, h: (first(J_CK) + h, b, 0)),
                  pl.BlockSpec(heads, lambda b, h: (first(J_CV) + h, b, 0)),
                  pl.BlockSpec((S, LANES), lambda b, h: (b, 0)),
                  pl.BlockSpec((hps, 1, S), lambda b, h: (b * (FA_HEADS // hps) + h, 0, 0))],
        out_specs=pl.BlockSpec(heads, lambda b, h: (h, b, 0)),
        out_shape=jax.ShapeDtypeStruct((FA_HEADS, T, HEAD_COLS), BF16),
        scratch_shapes=[pltpu.VMEM((hps, S, HEAD_COLS), BF16)],
        compiler_params=_cparams(("parallel", "parallel")),
        name="fox_attn",
    )(proj, proj, proj, cum, cum_t)


def _pool_kernel(h_ref, w_ref, sc_ref, o_ref):
    S = h_ref.shape[1]
    t = lax.broadcasted_iota(jnp.int32, (S, HEAD_COLS), 0)
    for g, win in enumerate(POOL_WINDOWS):
        cols = slice(g * HEAD_COLS, (g + 1) * HEAD_COLS)
        x = h_ref[g].astype(F32)
        tot = x
        span = 1
        while span < win:
            tot = tot + jnp.where(t >= span, pltpu.roll(tot, span, 0), 0.0)
            span *= 2
        cnt = jnp.minimum(t + 1, win).astype(F32)
        pooled = (tot / cnt - x).astype(BF16)
        y = jnp.dot(pooled, w_ref[g].astype(BF16), preferred_element_type=F32)
        o_ref[g] = (y * sc_ref[:, cols]).astype(o_ref.dtype)


def _pool(proj, w_pool, scale, B, S):
    T = proj.shape[1]
    grp = (SLOT_GROUPS, S, HEAD_COLS)
    return pl.pallas_call(
        _pool_kernel,
        grid=(B,),
        in_specs=[pl.BlockSpec(grp, lambda b: (J_DH, b, 0)),
                  pl.BlockSpec(w_pool.shape, lambda b: (0, 0, 0)),
                  pl.BlockSpec((1, SLOT), lambda b: (0, 0))],
        out_specs=pl.BlockSpec(grp, lambda b: (0, b, 0)),
        out_shape=jax.ShapeDtypeStruct((SLOT_GROUPS, T, HEAD_COLS), BF16),
        compiler_params=_cparams(("parallel",)),
        name="pool",
    )(proj, w_pool, scale)


def _merge_kernel(oa_ref, ob_ref, oc_ref, od_ref, gate_ref, wb_ref, wo_ref, g_ref, h_ref, out_ref):
    def wide(ref, g0, n):
        return jnp.concatenate([ref[g] for g in range(g0, g0 + n)], axis=1)

    per_branch = h_ref.shape[1] // HEAD_COLS
    merged = None
    for n, br in enumerate((oa_ref, ob_ref, oc_ref, od_ref)):
        bd = jnp.dot(wide(br, 0, SLOT_GROUPS), wb_ref[n], preferred_element_type=F32)
        term = jax.nn.sigmoid(wide(gate_ref, n * per_branch, per_branch).astype(F32)) * bd
        merged = term if merged is None else merged + term
    y = jnp.dot(merged.astype(BF16), wo_ref[...], preferred_element_type=F32)
    out_ref[...] = h_ref[...] + _rms(y, g_ref[...])


def _merge(branches, gates, w_branch, w_out, g, h, tm=MERGE_TM):
    T, D = h.shape
    row = lambda i: (i, 0)
    once = pl.Buffered(1)
    return pl.pallas_call(
        _merge_kernel,
        grid=(T // tm,),
        in_specs=[pl.BlockSpec((SLOT_GROUPS, tm, HEAD_COLS), lambda i: (0, i, 0))] * N_BRANCH + [
            pl.BlockSpec((N_BRANCH * D // HEAD_COLS, tm, HEAD_COLS), lambda i: (0, i, 0)),
            pl.BlockSpec(w_branch.shape, lambda i: (0, 0, 0), pipeline_mode=once),
            pl.BlockSpec(w_out.shape, lambda i: (0, 0), pipeline_mode=once),
            pl.BlockSpec((1, D), lambda i: (0, 0)),
            pl.BlockSpec((tm, D), row)],
        out_specs=pl.BlockSpec((tm, D), row),
        out_shape=jax.ShapeDtypeStruct((T, D), F32),
        compiler_params=_cparams(("parallel",), MERGE_VMEM_MB),
        name="merge_out",
    )(*branches, gates, w_branch, w_out, g, h)


def _ffn_kernel(h_ref, g1_ref, wu_ref, wd_ref, g2_ref, out_ref, hn_ref, acc_ref):
    f = pl.program_id(1)

    @pl.when(f == 0)
    def _():
        hn_ref[...] = _rms(h_ref[...], g1_ref[...]).astype(BF16)
        acc_ref[...] = jnp.zeros_like(acc_ref)

    up = jnp.dot(hn_ref[...], wu_ref[...], preferred_element_type=F32)
    a = jnp.square(jnp.maximum(up, 0.0)).astype(BF16)
    acc_ref[...] += jnp.dot(a, wd_ref[...], preferred_element_type=F32)

    @pl.when(f == pl.num_programs(1) - 1)
    def _():
        g2 = g2_ref[...]
        for r in _row_chunks(h_ref.shape[0]):
            out_ref[r, :] = h_ref[r, :] + _rms(acc_ref[r, :], g2)


def _ffn(h, g1, w_up, w_down, g2, tm=FFN_TM, tf=FFN_TF):
    T, D = h.shape
    F = w_up.shape[1]
    row = lambda i, f: (i, 0)
    vec = lambda i, f: (0, 0)
    return pl.pallas_call(
        _ffn_kernel,
        grid=(T // tm, F // tf),
        in_specs=[pl.BlockSpec((tm, D), row),
                  pl.BlockSpec((1, D), vec),
                  pl.BlockSpec((D, tf), lambda i, f: (0, f)),
                  pl.BlockSpec((tf, D), lambda i, f: (f, 0)),
                  pl.BlockSpec((1, D), vec)],
        out_specs=pl.BlockSpec((tm, D), row),
        out_shape=jax.ShapeDtypeStruct((T, D), F32),
        scratch_shapes=[pltpu.VMEM((tm, D), BF16), pltpu.VMEM((tm, D), F32)],
        compiler_params=_cparams(("parallel", "arbitrary"), FFN_VMEM_MB),
        name="ffn",
    )(h, g1, w_up, w_down, g2)


def kernel(x, positions, norm_mix_pre, norm_mix_post, norm_ffn_pre, norm_ffn_post, w_in, gm_ln_g,
           gm_ln_b, gm_w_s, gm_b_s, da_lambda, da_subln_g, fa_b_f, pool_w, pool_scale, w_branch,
           w_out, w_ffn_up, w_ffn_down):
    B, S, D = x.shape
    T = B * S
    h = x.reshape(T, D)
    tables = _rope_tables(positions)
    w_t, wf_t = _prep_w_in(w_in)
    L, NB, W, _ = w_branch.shape
    to_cast = (w_branch.reshape(L, NB * W, D), w_out, w_ffn_up, w_ffn_down)
    row = lambda a: a.reshape(1, -1)
    for l in range(DEPTH):
        lambda_init = 0.8 - 0.6 * math.exp(-0.3 * l)
        proj, f_logit, (wb, wo, wu, wd) = _inproj(h, row(norm_mix_pre[l]), w_t, wf_t, l, to_cast)

        o_a = _gmlp(proj, row(gm_ln_g[l]), row(gm_ln_b[l]), gm_w_s[l], gm_b_s[l].T)
        o_b = _diff_attn(proj, tables, da_lambda[l], row(da_subln_g[l]), B, S, lambda_init)
        b_f = jnp.pad(fa_b_f[l], (0, LANES - FA_HEADS)).reshape(1, LANES)
        cum, cum_t = _fcum(f_logit, b_f, B, S)
        o_c = _fox_attn(proj, cum, cum_t, B, S)
        o_d = _pool(proj, pool_w[l], row(pool_scale[l]), B, S)

        h = _merge((o_a, o_b, o_c, o_d), proj, wb.reshape(NB, W, D), wo, row(norm_mix_post[l]), h)
        h = _ffn(h, row(norm_ffn_pre[l]), wu, wd, row(norm_ffn_post[l]))
    return h.reshape(B, S, D)
```

```python
import functools
import math

import jax
import jax.numpy as jnp
from jax import lax
from jax.experimental import pallas as pl
from jax.experimental.pallas import tpu as pltpu

F32 = jnp.float32
BF16 = jnp.bfloat16

DEPTH = 2
N_BRANCH = 4
GM_GROUPS = 4
GM_CHUNK = 128
DA_HEADS = 4
DA_QK_DIM = 64
DA_ROT_DIM = 16
FA_HEADS = 4
FA_HEAD_DIM = 128
POOL_WINDOWS = (2, 4, 8, 16)
ROPE_THETA = 500000.0
NORM_EPS = 1e-6
MASK_VALUE = -1e30
LOG2E = math.log2(math.e)

LANES = 128
SUBLANES = 8
ROW_CHUNK = 2 * SUBLANES
CAST_BLOCKS = 64
HEAD_COLS = 128
SLOT = 512
SLOT_GROUPS = SLOT // HEAD_COLS
ATTN_HEADS_PER_STEP = 2

ROPE_TM = 1024
PREP_VMEM_MB = 40
INPROJ_TM, INPROJ_TN, INPROJ_VMEM_MB = 1024, 1280, 56
GMLP_TM = 2048
ATTN_TQ = 256
FCUM_BLOCK = 256
MERGE_TM, MERGE_VMEM_MB = 256, 48
FFN_TM, FFN_TF, FFN_VMEM_MB = 512, 1024, 48


N_GATE_SLOTS = 16
J_U, J_V, J_BQ, J_BK, J_BV, J_CQ, J_CK, J_CV, J_DH = range(N_GATE_SLOTS, N_GATE_SLOTS + 9)
N_SLOTS = J_DH + 1
N_ALIGNED = 8


def _rms(x, g):
    return x * lax.rsqrt(jnp.mean(x * x, axis=-1, keepdims=True) + NORM_EPS) * g


def _row_chunks(n_rows, rows=ROW_CHUNK):
    return [pl.ds(r0, rows) for r0 in range(0, n_rows, rows)]


def _next_row_tile(x_hbm, buf, sem):
    i, n = pl.program_id(0), pl.num_programs(0)
    tm = buf.shape[1]

    def copy(tile, slot):
        rows = pl.ds(pl.multiple_of(tile * tm, tm), tm)
        return pltpu.make_async_copy(x_hbm.at[rows, :], buf.at[slot], sem.at[slot])

    slot = i % 2

    @pl.when(i == 0)
    def _():
        copy(0, 0).start()

    copy(i, slot).wait()

    @pl.when(i + 1 < n)
    def _():
        copy(i + 1, 1 - slot).start(priority=1)

    return slot


def _cparams(sem, vmem_mb=None):
    kw = dict(dimension_semantics=sem)
    if vmem_mb is not None:
        kw["vmem_limit_bytes"] = vmem_mb * 1024 * 1024
    return pltpu.CompilerParams(**kw)


def _rope_kernel(pos_ref, inv_ref, c_ref, sa_ref, sb_ref):
    ang = pos_ref[...].astype(F32) * inv_ref[...]
    sub = lax.broadcasted_iota(jnp.int32, ang.shape, 1) % DA_QK_DIM
    half = DA_ROT_DIM // 2
    s = jnp.sin(ang)
    c_ref[...] = jnp.cos(ang)
    sa_ref[...] = jnp.where(sub < half, -s, 0.0)
    sb_ref[...] = jnp.where((sub >= half) & (sub < 2 * half), s, 0.0)


def _rope_tables(positions):
    T = positions.size
    tm = ROPE_TM
    half = DA_ROT_DIM // 2
    inv = 1.0 / (ROPE_THETA ** (jnp.arange(0, DA_ROT_DIM, 2, dtype=F32) / DA_ROT_DIM))
    sub = jnp.arange(LANES) % DA_QK_DIM
    inv_lane = jnp.where(sub < 2 * half, inv[sub % half], 0.0).reshape(1, LANES)
    tab = jax.ShapeDtypeStruct((T, LANES), F32)
    return pl.pallas_call(
        _rope_kernel,
        grid=(T // tm,),
        in_specs=[pl.BlockSpec((tm, 1), lambda i: (i, 0)),
                  pl.BlockSpec((1, LANES), lambda i: (0, 0))],
        out_specs=[pl.BlockSpec((tm, LANES), lambda i: (i, 0))] * 3,
        out_shape=[tab] * 3,
        compiler_params=_cparams(("parallel",)),
        name="rope_tables",
    )(positions.reshape(T, 1), inv_lane)


def _rotate(x, c, sa, sb):
    half = DA_ROT_DIM // 2
    return x * c + pltpu.roll(x, LANES - half, 1) * sa + pltpu.roll(x, half, 1) * sb


def _wprep_kernel(a_ref, o_ref):
    n_layers, tn, _ = o_ref.shape
    sub = lax.broadcasted_iota(jnp.int32, (SUBLANES, LANES), 0)
    low = {d: (sub & d) == 0 for d in (4, 2, 1)}

    def transpose8(a):
        for d in (4, 2, 1):
            nxt = list(a)
            for i in range(SUBLANES):
                if not i & d:
                    x, y = a[i], a[i + d]
                    nxt[i] = jnp.where(low[d], x, pltpu.roll(y, d, 0))
                    nxt[i + d] = jnp.where(low[d], pltpu.roll(x, SUBLANES - d, 0), y)
            a = nxt
        return a

    rows_per_iter = 2 * SUBLANES

    def body(it, carry):
        n0 = pl.multiple_of(it * rows_per_iter, rows_per_iter)
        for jb in range(a_ref.shape[1] // SUBLANES):
            js = pl.ds(jb * SUBLANES, SUBLANES)
            halves = [transpose8([a_ref[n0 + half * SUBLANES + i, js, :] for i in range(SUBLANES)])
                      for half in range(2)]
            for s in range(SUBLANES):
                kb, l = divmod(jb * SUBLANES + s, n_layers)
                tile = jnp.concatenate([halves[0][s], halves[1][s]], axis=0)
                o_ref[l, pl.ds(n0, rows_per_iter), kb * LANES:(kb + 1) * LANES] = tile.astype(o_ref.dtype)
        return carry

    lax.fori_loop(0, tn // rows_per_iter, body, 0)


def _native_rows(w_in):
    L, D, N = w_in.shape
    v = jnp.transpose(w_in.reshape(L, D // LANES, LANES, N), (3, 1, 0, 2))
    return v.reshape(N, (D // LANES) * L, LANES)


def _prep_rows(w_rows, L, D, tn, n, offset):
    return pl.pallas_call(
        _wprep_kernel,
        grid=(n,),
        in_specs=[pl.BlockSpec((pl.Element(tn), pl.Element(w_rows.shape[1]), pl.Element(LANES)),
                               lambda j: (offset(j), 0, 0))],
        out_specs=pl.BlockSpec((L, tn, D), lambda j: (0, j, 0)),
        out_shape=jax.ShapeDtypeStruct((L, n * tn, D), BF16),
        compiler_params=_cparams(("parallel",), PREP_VMEM_MB),
        name="w_in_prep",
    )(w_rows)


def _prep_w_in(w_in):
    L, D, _ = w_in.shape
    f0 = N_ALIGNED * SLOT
    d0 = f0 + FA_HEADS
    g0 = d0 + SLOT

    def slot_row(j):
        return jnp.where(j < N_GATE_SLOTS, g0 + SLOT * j, jnp.where(j == J_DH, d0, SLOT * (j - N_GATE_SLOTS)))

    w_rows = _native_rows(w_in)
    return (_prep_rows(w_rows, L, D, SLOT, N_SLOTS, slot_row),
            _prep_rows(w_rows, L, D, LANES, 1, lambda j: f0 + 0 * j))


def _dot_nt(a, b):
    return lax.dot_general(a, b, (((1,), (1,)), ((), ())), preferred_element_type=F32)


def _inproj_kernel(*refs, n_cast):
    h_hbm, g_ref, w_ref, wf_ref = refs[:4]
    cast_in = refs[4:4 + n_cast]
    out_ref, f_ref = refs[4 + n_cast:6 + n_cast]
    cast_out = refs[6 + n_cast:6 + 2 * n_cast]
    xn_ref, h_buf, h_sem = refs[-3:]

    @pl.when(pl.program_id(1) == 0)
    def _():
        slot = _next_row_tile(h_hbm, h_buf, h_sem)
        xn_ref[...] = _rms(h_buf[slot], g_ref[...]).astype(BF16)
        f_ref[...] = _dot_nt(xn_ref[...], wf_ref[...])

    r = _dot_nt(xn_ref[...], w_ref[...])
    for c in range(out_ref.shape[0]):
        out_ref[c] = r[:, c * LANES:(c + 1) * LANES].astype(out_ref.dtype)
    for src, dst in zip(cast_in, cast_out):
        dst[...] = src[...].astype(dst.dtype)


def _inproj(h, g, w_t, wf_t, l, to_cast, tm=INPROJ_TM, tn=INPROJ_TN):
    T, D = h.shape
    N = w_t.shape[1]
    ni, nj = T // tm, N // tn
    n_blk = CAST_BLOCKS
    assert ni * nj >= n_blk
    row = lambda i, j: (i, 0)
    blk = lambda i, j: jnp.minimum(i * nj + j, n_blk - 1)
    cast_in, cast_out, cast_shape = [], [], []
    for w in to_cast:
        _, R, C = w.shape
        cast_in.append(pl.BlockSpec((None, R // n_blk, C), lambda i, j: (l, blk(i, j), 0)))
        cast_out.append(pl.BlockSpec((R // n_blk, C), lambda i, j: (blk(i, j), 0)))
        cast_shape.append(jax.ShapeDtypeStruct((R, C), BF16))
    res = pl.pallas_call(
        functools.partial(_inproj_kernel, n_cast=len(to_cast)),
        grid=(ni, nj),
        in_specs=[pl.BlockSpec(memory_space=pl.ANY),
                  pl.BlockSpec((1, D), lambda i, j: (0, 0)),
                  pl.BlockSpec((None, tn, D), lambda i, j: (l, j, 0)),
                  pl.BlockSpec((None, LANES, D), lambda i, j: (l, 0, 0))] + cast_in,
        out_specs=[pl.BlockSpec((tn // LANES, tm, LANES), lambda i, j: (j, i, 0)),
                   pl.BlockSpec((tm, LANES), row)] + cast_out,
        out_shape=[jax.ShapeDtypeStruct((N // LANES, T, LANES), BF16),
                   jax.ShapeDtypeStruct((T, LANES), F32)] + cast_shape,
        scratch_shapes=[pltpu.VMEM((tm, D), BF16), pltpu.VMEM((2, tm, D), F32), pltpu.SemaphoreType.DMA((2,))],
        compiler_params=_cparams(("arbitrary", "arbitrary"), INPROJ_VMEM_MB),
        name="inproj",
    )(h, g, w_t, wf_t, *to_cast)
    return res[0], res[1], res[2:]


def _gmlp_kernel(u_ref, v_ref, lng_ref, lnb_ref, ws_ref, bst_ref, o_ref):
    v = jnp.concatenate([v_ref[g] for g in range(GM_GROUPS)], axis=1).astype(F32)
    mu = jnp.mean(v, axis=-1, keepdims=True)
    vc = v - mu
    var = jnp.mean(vc * vc, axis=-1, keepdims=True)
    vn = (vc * lax.rsqrt(var + NORM_EPS) * lng_ref[...] + lnb_ref[...]).astype(BF16)
    r = lax.broadcasted_iota(jnp.int32, (GM_CHUNK, GM_CHUNK), 0)
    c = lax.broadcasted_iota(jnp.int32, (GM_CHUNK, GM_CHUNK), 1)
    causal = r >= c
    for g in range(GM_GROUPS):
        wg = jnp.where(causal, ws_ref[g], 0.0).astype(BF16)
        bcol = bst_ref[:, g:g + 1]
        cols = slice(g * HEAD_COLS, (g + 1) * HEAD_COLS)
        for n in range(v.shape[0] // GM_CHUNK):
            rows = slice(n * GM_CHUNK, (n + 1) * GM_CHUNK)
            mixed = jnp.dot(wg, vn[rows, cols], preferred_element_type=F32) + bcol
            o_ref[g, rows, :] = (u_ref[g, rows, :].astype(F32) * mixed).astype(o_ref.dtype)


def _gmlp(proj, ln_g, ln_b, w_s, b_s_t, tm=GMLP_TM):
    T = proj.shape[1]
    const2 = lambda i: (0, 0)
    grp = (SLOT_GROUPS, tm, HEAD_COLS)
    return pl.pallas_call(
        _gmlp_kernel,
        grid=(T // tm,),
        in_specs=[pl.BlockSpec(grp, lambda i: (J_U, i, 0)),
                  pl.BlockSpec(grp, lambda i: (J_V, i, 0)),
                  pl.BlockSpec((1, SLOT), const2),
                  pl.BlockSpec((1, SLOT), const2),
                  pl.BlockSpec((GM_GROUPS, GM_CHUNK, GM_CHUNK), lambda i: (0, 0, 0)),
                  pl.BlockSpec((GM_CHUNK, GM_GROUPS), const2)],
        out_specs=pl.BlockSpec(grp, lambda i: (0, i, 0)),
        out_shape=jax.ShapeDtypeStruct((SLOT_GROUPS, T, HEAD_COLS), BF16),
        compiler_params=_cparams(("parallel",)),
        name="gmlp",
    )(proj, proj, ln_g, ln_b, w_s, b_s_t)


def _qk(q, k):
    return lax.dot_general(q, k, (((1,), (1,)), ((), ())), preferred_element_type=F32)


def _causal(rows, tq):
    r = lax.broadcasted_iota(jnp.int32, (rows, tq), 0) % tq
    c = lax.broadcasted_iota(jnp.int32, (rows, tq), 1)
    return r >= c


def _long_short_order(n):
    lo, hi, out = 0, n - 1, []
    while lo <= hi:
        out.append(hi)
        hi -= 1
        if lo <= hi:
            out.append(lo)
            lo += 1
    return out


def _pipelined_blocks(order, logits, weights, finish):
    assert len(order) >= 2
    s_q = [logits(i) for i in order[:2]]
    w_q = [weights(order[0], s_q.pop(0))]
    for n, i in enumerate(order):
        if n + 2 < len(order):
            s_q.append(logits(order[n + 2]))
        if n + 1 < len(order):
            w_q.append(weights(order[n + 1], s_q.pop(0)))
        finish(i, *w_q.pop(0))


def _mask_last_block(s, causal):
    tq = causal.shape[1]
    diag = jnp.where(causal, s[:, -tq:], MASK_VALUE)
    return diag if s.shape[1] == tq else jnp.concatenate([s[:, :-tq], diag], axis=1)


def _row_softmax(s, row_const=None):
    m = jnp.max(s, axis=-1, keepdims=True)
    if row_const is not None:
        m = (m + row_const) - row_const
    p = jnp.exp2(s - m)
    return p, jnp.sum(p, axis=-1, keepdims=True)


def _diff_attn_kernel(q_ref, k_ref, v_ref, c_ref, sa_ref, sb_ref, lam_ref, g_ref, o_ref,
                      qs_ref, kr_ref, *, tq, lambda_init):
    n_heads, S, _ = q_ref.shape
    c, sa, sb = c_ref[...], sa_ref[...], sb_ref[...]
    lp = lam_ref[...]

    def total(x):
        return jnp.sum(jnp.sum(x, axis=1, keepdims=True), axis=0, keepdims=True)

    lam = jnp.exp(total(lp[0:1] * lp[1:2])) - jnp.exp(total(lp[2:3] * lp[3:4])) + lambda_init
    causal = _causal(2 * tq, tq)
    g = g_ref[...]
    lane = lax.broadcasted_iota(jnp.int32, (S, HEAD_COLS), 1)

    for hh in range(n_heads):
        qf = _rotate(q_ref[hh].astype(F32), c, sa, sb) * (DA_QK_DIM ** -0.5 * LOG2E)
        qs_ref[hh, 0] = jnp.where(lane < DA_QK_DIM, qf, 0.0).astype(BF16)
        qs_ref[hh, 1] = jnp.where(lane >= DA_QK_DIM, qf, 0.0).astype(BF16)
        kr_ref[hh] = _rotate(k_ref[hh].astype(F32), c, sa, sb).astype(BF16)

        def logits(i, hh=hh):
            rows = pl.ds(i * tq, tq)
            q2 = jnp.concatenate([qs_ref[hh, 0, rows, :], qs_ref[hh, 1, rows, :]], axis=0)
            return _mask_last_block(_qk(q2, kr_ref[hh, pl.ds(0, (i + 1) * tq), :]), causal)

        def weights(i, s):
            p, l = _row_softmax(s)
            ratio = lam * l[:tq] / l[tq:]
            return (p[:tq] - ratio * p[tq:]).astype(BF16), l[:tq]

        def finish(i, w, l1, hh=hh):
            o = jnp.dot(w, v_ref[hh, pl.ds(0, (i + 1) * tq), :], preferred_element_type=F32) / l1
            o_ref[hh, pl.ds(i * tq, tq), :] = (_rms(o, g) * (1.0 - lambda_init)).astype(o_ref.dtype)

        _pipelined_blocks(_long_short_order(S // tq), logits, weights, finish)


def _diff_attn(proj, tables, lam, subln_g, B, S, lambda_init, tq=ATTN_TQ):
    T = proj.shape[1]
    hps = ATTN_HEADS_PER_STEP
    seq = lambda b, h: (b, 0)
    heads = (hps, S, HEAD_COLS)
    first = lambda slot: slot * SLOT_GROUPS // hps
    kern = functools.partial(_diff_attn_kernel, tq=tq, lambda_init=lambda_init)
    return pl.pallas_call(
        kern,
        grid=(B, DA_HEADS // hps),
        in_specs=[pl.BlockSpec(heads, lambda b, h: (first(J_BQ) + h, b, 0)),
                  pl.BlockSpec(heads, lambda b, h: (first(J_BK) + h, b, 0)),
                  pl.BlockSpec(heads, lambda b, h: (first(J_BV) + h, b, 0)),
                  pl.BlockSpec((S, LANES), seq),
                  pl.BlockSpec((S, LANES), seq),
                  pl.BlockSpec((S, LANES), seq),
                  pl.BlockSpec(lam.shape, lambda b, h: (0, 0)),
                  pl.BlockSpec((1, HEAD_COLS), lambda b, h: (0, 0))],
        out_specs=pl.BlockSpec(heads, lambda b, h: (h, b, 0)),
        out_shape=jax.ShapeDtypeStruct((DA_HEADS, T, HEAD_COLS), BF16),
        scratch_shapes=[pltpu.VMEM((hps, 2, S, HEAD_COLS), BF16), pltpu.VMEM((hps, S, HEAD_COLS), BF16)],
        compiler_params=_cparams(("parallel", "parallel")),
        name="diff_attn",
    )(proj, proj, proj, *tables, lam, subln_g)


def _fcum_kernel(f_ref, bf_ref, cum_ref, cumt_ref, *, blk):
    x = f_ref[...] + bf_ref[...]
    logf = jnp.minimum(x, 0.0) - jnp.log1p(jnp.exp(-jnp.abs(x)))
    r = lax.broadcasted_iota(jnp.int32, (blk, blk), 0)
    c = lax.broadcasted_iota(jnp.int32, (blk, blk), 1)
    tri = (r >= c).astype(F32)
    carry = jnp.zeros((1, LANES), F32)
    for n in range(x.shape[0] // blk):
        part = jnp.dot(tri, logf[n * blk:(n + 1) * blk], preferred_element_type=F32,
                       precision=lax.Precision.HIGHEST) + carry
        cum_ref[n * blk:(n + 1) * blk, :] = part
        carry = part[blk - 1:blk, :]
    cum_t = cum_ref[...].T
    for hh in range(FA_HEADS):
        cumt_ref[hh] = cum_t[hh:hh + 1, :]


def _fcum(f_logit, b_f, B, S):
    T = f_logit.shape[0]
    return pl.pallas_call(
        functools.partial(_fcum_kernel, blk=FCUM_BLOCK),
        grid=(B,),
        in_specs=[pl.BlockSpec((S, LANES), lambda b: (b, 0)),
                  pl.BlockSpec((1, LANES), lambda b: (0, 0))],
        out_specs=[pl.BlockSpec((S, LANES), lambda b: (b, 0)),
                   pl.BlockSpec((FA_HEADS, 1, S), lambda b: (b, 0, 0))],
        out_shape=[jax.ShapeDtypeStruct((T, LANES), F32),
                   jax.ShapeDtypeStruct((B * FA_HEADS, 1, S), F32)],
        compiler_params=_cparams(("parallel",)),
        name="forget_cumsum",
    )(f_logit, b_f)


def _fox_attn_kernel(q_ref, k_ref, v_ref, cum_ref, cumt_ref, o_ref, qs_ref, *, tq):
    n_heads, S, _ = q_ref.shape
    causal = _causal(tq, tq)
    for hh in range(n_heads):
        head = pl.program_id(1) * n_heads + hh
        qs_ref[hh] = (q_ref[hh].astype(F32) * (FA_HEAD_DIM ** -0.5 * LOG2E)).astype(BF16)

        def logits(i, hh=hh):
            keys = pl.ds(0, (i + 1) * tq)
            s = _qk(qs_ref[hh, pl.ds(i * tq, tq), :], k_ref[hh, keys, :]) - cumt_ref[hh, :, keys] * LOG2E
            return _mask_last_block(s, causal)

        def weights(i, s, head=head):
            cum = cum_ref[pl.ds(i * tq, tq), :]
            lane = lax.broadcasted_iota(jnp.int32, cum.shape, 1)
            cq = jnp.sum(jnp.where(lane == head, cum, 0.0), axis=-1, keepdims=True) * LOG2E
            p, l = _row_softmax(s, cq)
            return p.astype(BF16), l

        def finish(i, p, l, hh=hh):
            o = jnp.dot(p, v_ref[hh, pl.ds(0, (i + 1) * tq), :], preferred_element_type=F32)
            o_ref[hh, pl.ds(i * tq, tq), :] = (o / l).astype(o_ref.dtype)

        _pipelined_blocks(_long_short_order(S // tq), logits, weights, finish)


def _fox_attn(proj, cum, cum_t, B, S, tq=ATTN_TQ):
    T = proj.shape[1]
    hps = ATTN_HEADS_PER_STEP
    heads = (hps, S, HEAD_COLS)
    first = lambda slot: slot * SLOT_GROUPS // hps
    return pl.pallas_call(
        functools.partial(_fox_attn_kernel, tq=tq),
        grid=(B, FA_HEADS // hps),
        in_specs=[pl.BlockSpec(heads, lambda b, h: (first(J_CQ) + h, b, 0)),
                  pl.BlockSpec(heads, lambda b, h: (first(J_CK) + h, b, 0)),
                  pl.BlockSpec(heads, lambda b, h: (first(J_CV) + h, b, 0)),
                  pl.BlockSpec((S, LANES), lambda b, h: (b, 0)),
                  pl.BlockSpec((hps, 1, S), lambda b, h: (b * (FA_HEADS // hps) + h, 0, 0))],
        out_specs=pl.BlockSpec(heads, lambda b, h: (h, b, 0)),
        out_shape=jax.ShapeDtypeStruct((FA_HEADS, T, HEAD_COLS), BF16),
        scratch_shapes=[pltpu.VMEM((hps, S, HEAD_COLS), BF16)],
        compiler_params=_cparams(("parallel", "parallel")),
        name="fox_attn",
    )(proj, proj, proj, cum, cum_t)


def _pool_kernel(h_ref, w_ref, sc_ref, o_ref):
    S = h_ref.shape[1]
    t = lax.broadcasted_iota(jnp.int32, (S, HEAD_COLS), 0)
    for g, win in enumerate(POOL_WINDOWS):
        cols = slice(g * HEAD_COLS, (g + 1) * HEAD_COLS)
        x = h_ref[g].astype(F32)
        tot = x
        span = 1
        while span < win:
            tot = tot + jnp.where(t >= span, pltpu.roll(tot, span, 0), 0.0)
            span *= 2
        cnt = jnp.minimum(t + 1, win).astype(F32)
        pooled = (tot / cnt - x).astype(BF16)
        y = jnp.dot(pooled, w_ref[g].astype(BF16), preferred_element_type=F32)
        o_ref[g] = (y * sc_ref[:, cols]).astype(o_ref.dtype)


def _pool(proj, w_pool, scale, B, S):
    T = proj.shape[1]
    grp = (SLOT_GROUPS, S, HEAD_COLS)
    return pl.pallas_call(
        _pool_kernel,
        grid=(B,),
        in_specs=[pl.BlockSpec(grp, lambda b: (J_DH, b, 0)),
                  pl.BlockSpec(w_pool.shape, lambda b: (0, 0, 0)),
                  pl.BlockSpec((1, SLOT), lambda b: (0, 0))],
        out_specs=pl.BlockSpec(grp, lambda b: (0, b, 0)),
        out_shape=jax.ShapeDtypeStruct((SLOT_GROUPS, T, HEAD_COLS), BF16),
        compiler_params=_cparams(("parallel",)),
        name="pool",
    )(proj, w_pool, scale)


def _merge_kernel(oa_ref, ob_ref, oc_ref, od_ref, gate_ref, wb_ref, wo_ref, g_ref, h_ref, out_ref):
    def wide(ref, g0, n):
        return jnp.concatenate([ref[g] for g in range(g0, g0 + n)], axis=1)

    per_branch = h_ref.shape[1] // HEAD_COLS
    merged = None
    for n, br in enumerate((oa_ref, ob_ref, oc_ref, od_ref)):
        bd = jnp.dot(wide(br, 0, SLOT_GROUPS), wb_ref[n], preferred_element_type=F32)
        term = jax.nn.sigmoid(wide(gate_ref, n * per_branch, per_branch).astype(F32)) * bd
        merged = term if merged is None else merged + term
    y = jnp.dot(merged.astype(BF16), wo_ref[...], preferred_element_type=F32)
    out_ref[...] = h_ref[...] + _rms(y, g_ref[...])


def _merge(branches, gates, w_branch, w_out, g, h, tm=MERGE_TM):
    T, D = h.shape
    row = lambda i: (i, 0)
    once = pl.Buffered(1)
    return pl.pallas_call(
        _merge_kernel,
        grid=(T // tm,),
        in_specs=[pl.BlockSpec((SLOT_GROUPS, tm, HEAD_COLS), lambda i: (0, i, 0))] * N_BRANCH + [
            pl.BlockSpec((N_BRANCH * D // HEAD_COLS, tm, HEAD_COLS), lambda i: (0, i, 0)),
            pl.BlockSpec(w_branch.shape, lambda i: (0, 0, 0), pipeline_mode=once),
            pl.BlockSpec(w_out.shape, lambda i: (0, 0), pipeline_mode=once),
            pl.BlockSpec((1, D), lambda i: (0, 0)),
            pl.BlockSpec((tm, D), row)],
        out_specs=pl.BlockSpec((tm, D), row),
        out_shape=jax.ShapeDtypeStruct((T, D), F32),
        compiler_params=_cparams(("parallel",), MERGE_VMEM_MB),
        name="merge_out",
    )(*branches, gates, w_branch, w_out, g, h)


def _ffn_kernel(h_ref, g1_ref, wu_ref, wd_ref, g2_ref, out_ref, hn_ref, acc_ref):
    f = pl.program_id(1)

    @pl.when(f == 0)
    def _():
        hn_ref[...] = _rms(h_ref[...], g1_ref[...]).astype(BF16)
        acc_ref[...] = jnp.zeros_like(acc_ref)

    up = jnp.dot(hn_ref[...], wu_ref[...], preferred_element_type=F32)
    a = jnp.square(jnp.maximum(up, 0.0)).astype(BF16)
    acc_ref[...] += jnp.dot(a, wd_ref[...], preferred_element_type=F32)

    @pl.when(f == pl.num_programs(1) - 1)
    def _():
        g2 = g2_ref[...]
        for r in _row_chunks(h_ref.shape[0]):
            out_ref[r, :] = h_ref[r, :] + _rms(acc_ref[r, :], g2)


def _ffn(h, g1, w_up, w_down, g2, tm=FFN_TM, tf=FFN_TF):
    T, D = h.shape
    F = w_up.shape[1]
    row = lambda i, f: (i, 0)
    vec = lambda i, f: (0, 0)
    return pl.pallas_call(
        _ffn_kernel,
        grid=(T // tm, F // tf),
        in_specs=[pl.BlockSpec((tm, D), row),
                  pl.BlockSpec((1, D), vec),
                  pl.BlockSpec((D, tf), lambda i, f: (0, f)),
                  pl.BlockSpec((tf, D), lambda i, f: (f, 0)),
                  pl.BlockSpec((1, D), vec)],
        out_specs=pl.BlockSpec((tm, D), row),
        out_shape=jax.ShapeDtypeStruct((T, D), F32),
        scratch_shapes=[pltpu.VMEM((tm, D), BF16), pltpu.VMEM((tm, D), F32)],
        compiler_params=_cparams(("parallel", "arbitrary"), FFN_VMEM_MB),
        name="ffn",
    )(h, g1, w_up, w_down, g2)


def kernel(x, positions, norm_mix_pre, norm_mix_post, norm_ffn_pre, norm_ffn_post, w_in, gm_ln_g,
           gm_ln_b, gm_w_s, gm_b_s, da_lambda, da_subln_g, fa_b_f, pool_w, pool_scale, w_branch,
           w_out, w_ffn_up, w_ffn_down):
    B, S, D = x.shape
    T = B * S
    h = x.reshape(T, D)
    tables = _rope_tables(positions)
    w_t, wf_t = _prep_w_in(w_in)
    L, NB, W, _ = w_branch.shape
    to_cast = (w_branch.reshape(L, NB * W, D), w_out, w_ffn_up, w_ffn_down)
    row = lambda a: a.reshape(1, -1)
    for l in range(DEPTH):
        lambda_init = 0.8 - 0.6 * math.exp(-0.3 * l)
        proj, f_logit, (wb, wo, wu, wd) = _inproj(h, row(norm_mix_pre[l]), w_t, wf_t, l, to_cast)

        o_a = _gmlp(proj, row(gm_ln_g[l]), row(gm_ln_b[l]), gm_w_s[l], gm_b_s[l].T)
        o_b = _diff_attn(proj, tables, da_lambda[l], row(da_subln_g[l]), B, S, lambda_init)
        b_f = jnp.pad(fa_b_f[l], (0, LANES - FA_HEADS)).reshape(1, LANES)
        cum, cum_t = _fcum(f_logit, b_f, B, S)
        o_c = _fox_attn(proj, cum, cum_t, B, S)
        o_d = _pool(proj, pool_w[l], row(pool_scale[l]), B, S)

        h = _merge((o_a, o_b, o_c, o_d), proj, wb.reshape(NB, W, D), wo, row(norm_mix_post[l]), h)
        h = _ffn(h, row(norm_ffn_pre[l]), wu, wd, row(norm_ffn_post[l]))
    return h.reshape(B, S, D)
```
